```python
import jax, jax.numpy as jnp
from jax import lax
import numpy as np

D_MODEL = 1024
BATCH = 8
SEQ = 2048
DEPTH = 1
DEC_BATCH = 128
DEC_SEQ = 8
PAST_LEN = 16384
PAGE_SIZE = 128

POOL_WIDTH = D_MODEL // 2
POOL_GROUPS = 4
POOL_GROUP_DIM = POOL_WIDTH // POOL_GROUPS
POOL_WINDOWS = (2, 4, 8, 16)
POOL_BUF = 16 - 1
HGRN_WIDTH = D_MODEL // 2
HGRN_EXPAND = 128
HGRN_HEADS = HGRN_WIDTH // HGRN_EXPAND
HGRN_DK = HGRN_EXPAND
HGRN_DV = HGRN_WIDTH // HGRN_HEADS
CHUNK = 16
D_FF = -(-8 * D_MODEL // (3 * 256)) * 256
P_DIM = 256
EPS = 1e-6
IN_WIDTH = POOL_WIDTH + 4 * HGRN_WIDTH + 2 * D_MODEL

kernel_name = 'pool_hgrn2_gated_hybrid_step'


def rms_norm(x, gain):
    xf = x.astype(jnp.float32)
    y = xf * lax.rsqrt(jnp.mean(xf * xf, axis=-1, keepdims=True) + EPS)
    return (y * gain.astype(jnp.float32)).astype(x.dtype)


def multiscale_pool(u):
    B, T, C = u.shape
    cs = jnp.concatenate([jnp.zeros((B, 1, C), u.dtype), jnp.cumsum(u, axis=1)], axis=1)
    idx = jnp.arange(T)
    means = []
    for g, w in enumerate(POOL_WINDOWS):
        c0 = g * POOL_GROUP_DIM
        csg = cs[:, :, c0:c0 + POOL_GROUP_DIM]
        lo = jnp.maximum(idx + 1 - w, 0)
        cnt = jnp.minimum(idx + 1, w).astype(u.dtype)
        means.append((csg[:, idx + 1] - csg[:, lo]) / cnt[None, :, None])
    return jnp.concatenate(means, axis=-1) - u


def hgrn2_recurrence(q, k, v, logf, s0):
    B, T, H, _ = q.shape
    pad = (-T) % CHUNK
    if pad:
        pw = ((0, 0), (0, pad), (0, 0), (0, 0))
        q, k, v, logf = [jnp.pad(a, pw) for a in (q, k, v, logf)]
    n = (T + pad) // CHUNK

    def blocks(a):
        return a.reshape(B, n, CHUNK, H, a.shape[-1])

    q, k, v, logf = blocks(q), blocks(k), blocks(v), blocks(logf)
    G = jnp.cumsum(logf, axis=2)
    G_last = G[:, :, -1:]
    qe = q * jnp.exp(G)
    ke = k * jnp.exp(-G)
    kd = k * jnp.exp(G_last - G)
    causal = jnp.tril(jnp.ones((CHUNK, CHUNK), dtype=bool))
    A = jnp.einsum('bnthk,bnshk->bnhts', qe, ke)
    A = jnp.where(causal, A, 0.0)
    o_intra = jnp.einsum('bnhts,bnshv->bnthv', A, v)
    dS = jnp.einsum('bnshk,bnshv->bnhkv', kd, v)
    decay = jnp.exp(G_last[:, :, 0])

    def step(S, inp):
        qe_c, dec_c, dS_c = inp
        o_c = jnp.einsum('bthk,bhkv->bthv', qe_c, S)
        return dec_c[..., None] * S + dS_c, o_c

    S, o_inter = lax.scan(step, s0, (jnp.moveaxis(qe, 1, 0), jnp.moveaxis(decay, 1, 0), jnp.moveaxis(dS, 1, 0)))
    o = o_intra + jnp.moveaxis(o_inter, 0, 1)
    o = o.reshape(B, n * CHUNK, H, HGRN_DV)[:, :T]
    return o, S


def run_trunk(x, p, state_pool, state_hgrn, w):
    f32 = jnp.float32
    B, T, _ = x.shape
    dt = x.dtype
    out_dt = dt if state_pool is None else state_pool.dtype
    splits = [POOL_WIDTH, POOL_WIDTH + HGRN_WIDTH, POOL_WIDTH + 2 * HGRN_WIDTH,
              POOL_WIDTH + 3 * HGRN_WIDTH, POOL_WIDTH + 4 * HGRN_WIDTH,
              POOL_WIDTH + 4 * HGRN_WIDTH + D_MODEL]
    lb_all = jnp.cumsum(jax.nn.softmax(w['hgrn_lb'].astype(f32), axis=0), axis=0)
    new_pool, new_hgrn = [], []
    for l in range(DEPTH):
        h = rms_norm(x, w['g_mix'][l])
        z = (h @ w['w_in'][l]).astype(f32)
        u, zq, zf, zi, zg, ga, gb = jnp.split(z, splits, axis=-1)

        u_ext = u if state_pool is None else jnp.concatenate([state_pool[l].astype(f32), u], axis=1)
        new_pool.append(u_ext[:, -POOL_BUF:].astype(out_dt))
        pooled = multiscale_pool(u_ext)[:, -T:].reshape(B, T, POOL_GROUPS, POOL_GROUP_DIM)
        pool_out = jnp.einsum('btgc,gcd->btgd', pooled, w['w_pool_mix'][l].astype(f32))
        pool_out = pool_out.reshape(B, T, POOL_WIDTH) * w['pool_scale'][l].astype(f32)

        lb = lb_all[l]
        fg = lb + (1.0 - lb) * jax.nn.sigmoid(zf)
        heads = lambda a: a.reshape(B, T, HGRN_HEADS, -1)
        q = heads(jax.nn.silu(zq))
        k = heads(1.0 - fg)
        logf = heads(jnp.log(fg))
        v = heads(zi)
        s0 = jnp.zeros((B, HGRN_HEADS, HGRN_DK, HGRN_DV), f32) if state_hgrn is None else state_hgrn[l].astype(f32)
        o, s_new = hgrn2_recurrence(q, k, v, logf, s0)
        new_hgrn.append(s_new.astype(out_dt))
        o = o * lax.rsqrt(jnp.mean(o * o, axis=-1, keepdims=True) + EPS)
        o = o * w['hgrn_norm'][l].astype(f32).reshape(HGRN_HEADS, HGRN_DV)
        o = o.reshape(B, T, HGRN_WIDTH) * jax.nn.silu(zg)

        ya = pool_out.astype(dt) @ w['w_pool_up'][l]
        yb = o.astype(dt) @ w['w_hgrn_up'][l]
        merged = (jax.nn.sigmoid(ga) * ya.astype(f32) + jax.nn.sigmoid(gb) * yb.astype(f32)).astype(dt)
        x = x + merged @ w['w_out'][l]

        h2 = rms_norm(x, w['g_ffn'][l])
        x = x + (jax.nn.silu(h2 @ w['w_ffn_gate'][l]) * (h2 @ w['w_ffn_up'][l])) @ w['w_ffn_down'][l]

        h3 = rms_norm(x, w['g_ple'][l])
        gate = jax.nn.sigmoid((h3 @ w['w_ple_gate'][l]).astype(f32))
        emb = (p[l].astype(dt) @ w['w_ple_proj'][l]).astype(f32)
        x = x + (gate * emb).astype(dt)

    y = rms_norm(x, w['g_final'])
    return y, jnp.stack(new_pool, axis=0), jnp.stack(new_hgrn, axis=0)


def setup_inputs(seed: int = 0) -> dict:
    key = jax.random.key(seed)
    ks = jax.random.split(key, 24)
    f32 = jnp.float32

    def nrm(k, shape, scale):
        return jax.random.normal(k, shape, f32) * scale

    def gain(k, shape):
        return 1.0 + 0.05 * jax.random.normal(k, shape, f32)

    return {
        'x_prompt': nrm(ks[0], (BATCH, SEQ, D_MODEL), 1.0),
        'x_sample': nrm(ks[1], (DEC_BATCH, DEC_SEQ, D_MODEL), 1.0),
        'state_pool': nrm(ks[2], (DEPTH, DEC_BATCH, POOL_BUF, POOL_WIDTH), 1.0),
        'state_hgrn': nrm(ks[3], (DEPTH, DEC_BATCH, HGRN_HEADS, HGRN_DK, HGRN_DV), 0.5),
        'p_prompt': nrm(ks[4], (DEPTH, BATCH, SEQ, P_DIM), 1.0),
        'p_sample': nrm(ks[5], (DEPTH, DEC_BATCH, DEC_SEQ, P_DIM), 1.0),
        'g_mix': gain(ks[6], (DEPTH, D_MODEL)),
        'w_in': nrm(ks[7], (DEPTH, D_MODEL, IN_WIDTH), D_MODEL ** -0.5),
        'w_pool_mix': nrm(ks[8], (DEPTH, POOL_GROUPS, POOL_GROUP_DIM, POOL_GROUP_DIM), POOL_GROUP_DIM ** -0.5),
        'pool_scale': gain(ks[9], (DEPTH, POOL_WIDTH)),
        'hgrn_lb': nrm(ks[10], (DEPTH + 1, HGRN_WIDTH), 0.1),
        'hgrn_norm': gain(ks[11], (DEPTH, HGRN_WIDTH)),
        'w_pool_up': nrm(ks[12], (DEPTH, POOL_WIDTH, D_MODEL), POOL_WIDTH ** -0.5),
        'w_hgrn_up': nrm(ks[13], (DEPTH, HGRN_WIDTH, D_MODEL), HGRN_WIDTH ** -0.5),
        'w_out': nrm(ks[14], (DEPTH, D_MODEL, D_MODEL), D_MODEL ** -0.5),
        'g_ffn': gain(ks[15], (DEPTH, D_MODEL)),
        'w_ffn_gate': nrm(ks[16], (DEPTH, D_MODEL, D_FF), D_MODEL ** -0.5),
        'w_ffn_up': nrm(ks[17], (DEPTH, D_MODEL, D_FF), D_MODEL ** -0.5),
        'w_ffn_down': nrm(ks[18], (DEPTH, D_FF, D_MODEL), D_FF ** -0.5),
        'g_ple': gain(ks[19], (DEPTH, D_MODEL)),
        'w_ple_gate': nrm(ks[20], (DEPTH, D_MODEL, D_MODEL), D_MODEL ** -0.5),
        'w_ple_proj': nrm(ks[21], (DEPTH, P_DIM, D_MODEL), P_DIM ** -0.5),
        'g_final': gain(ks[22], (D_MODEL,)),
    }


def reference(x_prompt, x_sample, state_pool, state_hgrn, p_prompt, p_sample,
              g_mix, w_in, w_pool_mix, pool_scale, hgrn_lb, hgrn_norm,
              w_pool_up, w_hgrn_up, w_out, g_ffn, w_ffn_gate, w_ffn_up, w_ffn_down,
              g_ple, w_ple_gate, w_ple_proj, g_final):
    w = {
        'g_mix': g_mix, 'w_in': w_in, 'w_pool_mix': w_pool_mix, 'pool_scale': pool_scale,
        'hgrn_lb': hgrn_lb, 'hgrn_norm': hgrn_norm, 'w_pool_up': w_pool_up,
        'w_hgrn_up': w_hgrn_up, 'w_out': w_out, 'g_ffn': g_ffn, 'w_ffn_gate': w_ffn_gate,
        'w_ffn_up': w_ffn_up, 'w_ffn_down': w_ffn_down, 'g_ple': g_ple,
        'w_ple_gate': w_ple_gate, 'w_ple_proj': w_ple_proj, 'g_final': g_final,
    }
    y_prompt, new_pool_prompt, new_hgrn_prompt = run_trunk(x_prompt, p_prompt, None, None, w)
    y_sample, new_pool_sample, new_hgrn_sample = run_trunk(x_sample, p_sample, state_pool, state_hgrn, w)
    return (y_prompt, y_sample, new_pool_prompt, new_hgrn_prompt, new_pool_sample, new_hgrn_sample)
```

```python
import functools

import jax
import jax.numpy as jnp
from jax import lax
from jax.experimental import pallas as pl
from jax.experimental.pallas import tpu as pltpu

F32 = jnp.float32
BF16 = jnp.bfloat16

EPS = 1e-6
POOL_WINDOWS = (2, 4, 8, 16)
POOL_HALO = 16
SUB = 16
LANES = 128
VMEM_LIMIT = 56 * 1024 * 1024

_NT = (((1,), (1,)), ((), ()))
_TN = (((0,), (0,)), ((), ()))


def _bf(x):
    return x.astype(BF16)


def _dot(a, b):
    return jnp.dot(a, b, preferred_element_type=F32)


def _rms(x, gain):
    ms = jnp.mean(x * x, axis=-1, keepdims=True)
    return x * lax.rsqrt(ms + EPS) * gain


def _silu(x):
    return x * jax.nn.sigmoid(x)


def _forget_lower_bound(lb_ref):
    a0 = lb_ref[0:1, :]
    a1 = lb_ref[1:2, :]
    m = jnp.maximum(a0, a1)
    e0 = jnp.exp(a0 - m)
    e1 = jnp.exp(a1 - m)
    return e0 / (e0 + e1)


def _block_scans(logf, pos, blk):
    n = logf.shape[0]
    c = logf
    s = 1
    while s < blk:
        c = c + jnp.where(pos >= s, pltpu.roll(c, s, axis=0), 0.0)
        s *= 2
    r = jnp.where(pos <= blk - 2, pltpu.roll(logf, n - 1, axis=0), 0.0)
    s = 1
    while s < blk:
        r = r + jnp.where(pos + s <= blk - 1, pltpu.roll(r, n - s, axis=0), 0.0)
        s *= 2
    return c, r


def _head_step(levels, v, s_prev, q_big, k_big, dec_row):
    a = None
    for mask, q_l, k_l in levels:
        a_l = lax.dot_general(_bf(q_l), _bf(k_l), _NT, preferred_element_type=F32)
        a = jnp.where(mask, a_l, 0.0 if a is None else a)
    vb = _bf(v)
    o = _dot(_bf(a), vb) + _dot(_bf(q_big), _bf(s_prev))
    ds = lax.dot_general(_bf(k_big), vb, _TN, preferred_element_type=F32)
    dk = s_prev.shape[0]
    dec = jnp.transpose(jnp.broadcast_to(dec_row, (dk, dk)))
    return o, dec * s_prev + ds


def _gated_merge_out(x, h, merged_a, o_raw, zg, w_in_ref, hgrn_norm_ref, w_hgrn_up_ref, w_out_ref,
                     col_gb, n_heads):
    pieces = []
    for hh in range(n_heads):
        ln = slice(hh * LANES, (hh + 1) * LANES)
        oh = o_raw[:, ln]
        oh = oh * lax.rsqrt(jnp.mean(oh * oh, axis=-1, keepdims=True) + EPS)
        pieces.append(oh * hgrn_norm_ref[:, ln])
    o = jnp.concatenate(pieces, axis=-1) * _silu(zg)
    yb = _dot(_bf(o), w_hgrn_up_ref[...])
    gb = _dot(h, w_in_ref[:, col_gb:col_gb + x.shape[1]])
    merged = merged_a + jax.nn.sigmoid(gb) * yb
    return x + _dot(_bf(merged), w_out_ref[...])


def _pool_branch(h, u, shifted, inv_cnt, w_in_ref, w_pool_mix_ref, pool_scale_ref, w_pool_up_ref,
                 col_ga, d_model):
    pieces = []
    for g, w in enumerate(POOL_WINDOWS):
        ln = slice(g * LANES, (g + 1) * LANES)
        ug = u[:, ln]
        acc = ug
        for j in range(1, w):
            acc = acc + shifted(j, ln)
        pooled = acc * inv_cnt(w) - ug
        pieces.append(_dot(_bf(pooled), w_pool_mix_ref[g]))
    pool_out = jnp.concatenate(pieces, axis=-1) * pool_scale_ref[...]
    ya = _dot(_bf(pool_out), w_pool_up_ref[...])
    ga = _dot(h, w_in_ref[:, col_ga:col_ga + d_model])
    return jax.nn.sigmoid(ga) * ya


def _mixer_prompt_kernel(x_ref, g_mix_ref, w_in_ref, w_pool_mix_ref, pool_scale_ref, lb_ref,
                         hgrn_norm_ref, w_pool_up_ref, w_hgrn_up_ref, w_out_ref,
                         x1_ref, pool_out_ref, hgrn_out_ref,
                         u_scr, z_scr, o_scr, s_scr, *, chunk):
    tm, d_model = x_ref.shape
    pw = u_scr.shape[1]
    hw = o_scr.shape[1]
    n_heads = hw // LANES
    t = pl.program_id(1)

    @pl.when(t == 0)
    def _():
        u_scr[0:POOL_HALO, :] = jnp.zeros((POOL_HALO, pw), F32)
        s_scr[...] = jnp.zeros(s_scr.shape, F32)

    x = x_ref[...]
    h = _bf(_rms(x, g_mix_ref[...]))

    u = _dot(h, w_in_ref[:, 0:pw])
    u_scr[POOL_HALO:POOL_HALO + tm, :] = u
    seen = lax.broadcasted_iota(jnp.int32, (tm, 1), 0) + (t * tm + 1)

    def shifted(j, ln):
        return u_scr[pl.ds(POOL_HALO - j, tm), ln]

    def inv_cnt(w):
        return 1.0 / jnp.minimum(seen, w).astype(F32)

    col_ga = pw + 4 * hw
    merged_a = _pool_branch(h, u, shifted, inv_cnt, w_in_ref, w_pool_mix_ref, pool_scale_ref,
                            w_pool_up_ref, col_ga, d_model)
    u_scr[0:POOL_HALO, :] = u_scr[tm:tm + POOL_HALO, :]

    z_scr[...] = _dot(h, w_in_ref[:, pw:pw + 4 * hw])
    lb = _forget_lower_bound(lb_ref)

    row = lax.broadcasted_iota(jnp.int32, (chunk, 1), 0)
    ti = lax.broadcasted_iota(jnp.int32, (chunk, chunk), 0)
    si = lax.broadcasted_iota(jnp.int32, (chunk, chunk), 1)
    pos = row & (SUB - 1)
    masks = [((ti // SUB) == (si // SUB)) & (si <= ti)]
    bits = []
    b = SUB
    while b < chunk:
        masks.append(((ti // (2 * b)) == (si // (2 * b))) & ((ti & b) != 0) & ((si & b) == 0))
        bits.append((row & b) != 0)
        b *= 2

    def chunk_body(c, carry):
        r0 = pl.multiple_of(c * chunk, chunk)
        rows = pl.ds(r0, chunk)
        for hh in range(n_heads):
            ln = slice(hh * LANES, (hh + 1) * LANES)
            zq = z_scr[rows, hh * LANES:(hh + 1) * LANES]
            zf = z_scr[rows, hw + hh * LANES:hw + (hh + 1) * LANES]
            v = z_scr[rows, 2 * hw + hh * LANES:2 * hw + (hh + 1) * LANES]
            lbh = lb[:, ln]
            fg = lbh + (1.0 - lbh) * jax.nn.sigmoid(zf)
            k = 1.0 - fg
            logf = jnp.log(fg)
            q = _silu(zq)
            cs, rs = _block_scans(logf, pos, SUB)
            levels = [(masks[0], q * jnp.exp(cs), k * jnp.exp(-cs))]
            tot = cs + rs
            b = SUB
            for lvl, upper in enumerate(bits):
                levels.append((masks[lvl + 1], q * jnp.exp(cs), k * jnp.exp(rs)))
                prev_tot = pltpu.roll(tot, b, axis=0)
                next_tot = pltpu.roll(tot, chunk - b, axis=0)
                cs = cs + jnp.where(upper, prev_tot, 0.0)
                rs = rs + jnp.where(upper, 0.0, next_tot)
                tot = tot + jnp.where(upper, prev_tot, next_tot)
                b *= 2
            o, s_new = _head_step(levels, v, s_scr[hh], q * jnp.exp(cs), k * jnp.exp(rs),
                                  jnp.exp(tot[0:1, :]))
            s_scr[hh] = s_new
            o_scr[rows, ln] = o
        return carry

    lax.fori_loop(0, tm // chunk, chunk_body, 0)

    x1_ref[...] = _gated_merge_out(x, h, merged_a, o_scr[...], z_scr[:, 3 * hw:4 * hw], w_in_ref,
                                   hgrn_norm_ref, w_hgrn_up_ref, w_out_ref, col_ga + d_model, n_heads)

    @pl.when(t == pl.num_programs(1) - 1)
    def _():
        pool_out_ref[...] = u_scr[0:POOL_HALO, :]
        hgrn_out_ref[...] = s_scr[...]


def _mixer_sample_kernel(x_ref, sp_ref, sh_ref, g_mix_ref, w_in_ref, w_pool_mix_ref, pool_scale_ref,
                         lb_ref, hgrn_norm_ref, w_pool_up_ref, w_hgrn_up_ref, w_out_ref,
                         x1_ref, pool_out_ref, hgrn_out_ref,
                         e_scr, z_scr, o_scr, qe_scr, ke_scr, kd_scr, dec_scr, *, seq):
    rows_n, d_model = x_ref.shape
    tb, buf, pw = sp_ref.shape
    hw = o_scr.shape[1]
    n_heads = hw // LANES
    ext = e_scr.shape[1]

    x = x_ref[...]
    h = _bf(_rms(x, g_mix_ref[...]))

    u = _dot(h, w_in_ref[:, 0:pw])
    e_scr[:, ext - seq - buf:ext - seq, :] = sp_ref[...]
    e_scr[:, ext - seq:ext, :] = u.reshape(tb, seq, pw)

    def shifted(j, ln):
        return e_scr[:, pl.ds(ext - seq - j, seq), ln].reshape(rows_n, LANES)

    def inv_cnt(w):
        return 1.0 / w

    col_ga = pw + 4 * hw
    merged_a = _pool_branch(h, u, shifted, inv_cnt, w_in_ref, w_pool_mix_ref, pool_scale_ref,
                            w_pool_up_ref, col_ga, d_model)
    pool_out_ref[...] = e_scr[:, ext - POOL_HALO:ext, :]

    z_scr[...] = _dot(h, w_in_ref[:, pw:pw + 4 * hw])
    lb = _forget_lower_bound(lb_ref)
    pos = lax.broadcasted_iota(jnp.int32, (rows_n, 1), 0) & (seq - 1)
    for hh in range(n_heads):
        ln = slice(hh * LANES, (hh + 1) * LANES)
        zq = z_scr[:, hh * LANES:(hh + 1) * LANES]
        zf = z_scr[:, hw + hh * LANES:hw + (hh + 1) * LANES]
        lbh = lb[:, ln]
        fg = lbh + (1.0 - lbh) * jax.nn.sigmoid(zf)
        k = 1.0 - fg
        logf = jnp.log(fg)
        q = _silu(zq)
        cs, rs = _block_scans(logf, pos, seq)
        qe_scr[:, ln] = q * jnp.exp(cs)
        ke_scr[:, ln] = k * jnp.exp(-cs)
        kd_scr[:, ln] = k * jnp.exp(rs)
        dec_scr[:, ln] = jnp.exp(cs + rs)

    ti = lax.broadcasted_iota(jnp.int32, (seq, seq), 0)
    si = lax.broadcasted_iota(jnp.int32, (seq, seq), 1)
    causal = si <= ti

    def seq_body(b, carry):
        r0 = pl.multiple_of(b * seq, seq)
        rows = pl.ds(r0, seq)
        for hh in range(n_heads):
            ln = slice(hh * LANES, (hh + 1) * LANES)
            qe = qe_scr[rows, ln]
            v = z_scr[rows, 2 * hw + hh * LANES:2 * hw + (hh + 1) * LANES]
            o, s_new = _head_step([(causal, qe, ke_scr[rows, ln])], v, sh_ref[b, hh], qe,
                                  kd_scr[rows, ln], dec_scr[pl.ds(r0, 1), ln])
            hgrn_out_ref[b, hh] = s_new
            o_scr[rows, ln] = o
        return carry

    lax.fori_loop(0, tb, seq_body, 0)

    x1_ref[...] = _gated_merge_out(x, h, merged_a, o_scr[...], z_scr[:, 3 * hw:4 * hw], w_in_ref,
                                   hgrn_norm_ref, w_hgrn_up_ref, w_out_ref, col_ga + d_model, n_heads)


def _ffn_ple_kernel(x_ref, p_ref, g_ffn_ref, w_gate_ref, w_up_ref, w_down_ref, g_ple_ref,
                    w_ple_gate_ref, w_ple_proj_ref, g_final_ref, y_ref):
    x = x_ref[...]
    h2 = _bf(_rms(x, g_ffn_ref[...]))
    act = _silu(_dot(h2, w_gate_ref[...])) * _dot(h2, w_up_ref[...])
    x = x + _dot(_bf(act), w_down_ref[...])
    h3 = _bf(_rms(x, g_ple_ref[...]))
    gate = jax.nn.sigmoid(_dot(h3, w_ple_gate_ref[...]))
    emb = _dot(_bf(p_ref[...]), w_ple_proj_ref[...])
    x = x + gate * emb
    y_ref[...] = _rms(x, g_final_ref[...])


def _whole(_):
    return pl.BlockSpec(memory_space=pltpu.VMEM)


def _mixer_prompt(x, weights, *, tile, chunk):
    bsz, seq, d_model = x.shape
    pw = weights[3].shape[1]
    hw = weights[5].shape[1]
    n_heads = hw // LANES
    assert seq % tile == 0 and tile % chunk == 0 and chunk % SUB == 0
    return pl.pallas_call(
        functools.partial(_mixer_prompt_kernel, chunk=chunk),
        grid=(bsz, seq // tile),
        in_specs=[pl.BlockSpec((None, tile, d_model), lambda b, t: (b, t, 0))] + [_whole(w) for w in weights],
        out_specs=[
            pl.BlockSpec((None, tile, d_model), lambda b, t: (b, t, 0)),
            pl.BlockSpec((None, POOL_HALO, pw), lambda b, t: (b, 0, 0)),
            pl.BlockSpec((None, n_heads, LANES, LANES), lambda b, t: (b, 0, 0, 0)),
        ],
        out_shape=[
            jax.ShapeDtypeStruct((bsz, seq, d_model), F32),
            jax.ShapeDtypeStruct((bsz, POOL_HALO, pw), F32),
            jax.ShapeDtypeStruct((bsz, n_heads, LANES, LANES), F32),
        ],
        scratch_shapes=[
            pltpu.VMEM((POOL_HALO + tile, pw), F32),
            pltpu.VMEM((tile, 4 * hw), F32),
            pltpu.VMEM((tile, hw), F32),
            pltpu.VMEM((n_heads, LANES, LANES), F32),
        ],
        compiler_params=pltpu.CompilerParams(
            dimension_semantics=("parallel", "arbitrary"), vmem_limit_bytes=VMEM_LIMIT),
        name="mixer_prompt",
    )(x, *weights)


def _mixer_sample(x, state_pool, state_hgrn, weights, *, tile_b):
    bsz, seq, d_model = x.shape
    _, buf, pw = state_pool.shape
    _, n_heads, dk, dv = state_hgrn.shape
    hw = n_heads * dv
    assert bsz % tile_b == 0 and seq == 8 and buf + 1 == POOL_HALO and dk == LANES and dv == LANES
    rows = tile_b * seq
    xf = x.reshape(bsz * seq, d_model)
    return pl.pallas_call(
        functools.partial(_mixer_sample_kernel, seq=seq),
        grid=(bsz // tile_b,),
        in_specs=[
            pl.BlockSpec((rows, d_model), lambda i: (i, 0)),
            pl.BlockSpec((tile_b, buf, pw), lambda i: (i, 0, 0)),
            pl.BlockSpec((tile_b, n_heads, dk, dv), lambda i: (i, 0, 0, 0)),
        ] + [_whole(w) for w in weights],
        out_specs=[
            pl.BlockSpec((rows, d_model), lambda i: (i, 0)),
            pl.BlockSpec((tile_b, POOL_HALO, pw), lambda i: (i, 0, 0)),
            pl.BlockSpec((tile_b, n_heads, dk, dv), lambda i: (i, 0, 0, 0)),
        ],
        out_shape=[
            jax.ShapeDtypeStruct((bsz * seq, d_model), F32),
            jax.ShapeDtypeStruct((bsz, POOL_HALO, pw), F32),
            jax.ShapeDtypeStruct((bsz, n_heads, dk, dv), F32),
        ],
        scratch_shapes=[
            pltpu.VMEM((tile_b, 1 + buf + seq, pw), F32),
            pltpu.VMEM((rows, 4 * hw), F32),
            pltpu.VMEM((rows, hw), F32),
            pltpu.VMEM((rows, hw), F32),
            pltpu.VMEM((rows, hw), F32),
            pltpu.VMEM((rows, hw), F32),
            pltpu.VMEM((rows, hw), F32),
        ],
        compiler_params=pltpu.CompilerParams(
            dimension_semantics=("parallel",), vmem_limit_bytes=VMEM_LIMIT),
        name="mixer_sample",
    )(xf, state_pool, state_hgrn, *weights)


def _ffn_ple(x, p, weights, *, tile):
    n, d_model = x.shape
    p_dim = p.shape[1]
    assert n % tile == 0
    return pl.pallas_call(
        _ffn_ple_kernel,
        grid=(n // tile,),
        in_specs=[
            pl.BlockSpec((tile, d_model), lambda i: (i, 0)),
            pl.BlockSpec((tile, p_dim), lambda i: (i, 0)),
        ] + [_whole(w) for w in weights],
        out_specs=pl.BlockSpec((tile, d_model), lambda i: (i, 0)),
        out_shape=jax.ShapeDtypeStruct((n, d_model), F32),
        compiler_params=pltpu.CompilerParams(
            dimension_semantics=("parallel",), vmem_limit_bytes=VMEM_LIMIT),
        name="ffn_ple",
    )(x, p, *weights)


def kernel(x_prompt, x_sample, state_pool, state_hgrn, p_prompt, p_sample, g_mix, w_in, w_pool_mix, pool_scale, hgrn_lb, hgrn_norm, w_pool_up, w_hgrn_up, w_out, g_ffn, w_ffn_gate, w_ffn_up, w_ffn_down, g_ple, w_ple_gate, w_ple_proj, g_final):
    depth = w_in.shape[0]
    assert depth == 1 and hgrn_lb.shape[0] == 2
    bsz, seq, d_model = x_prompt.shape
    dbsz, dseq, _ = x_sample.shape
    buf = state_pool.shape[2]

    mixer_w = (g_mix, _bf(w_in[0]), _bf(w_pool_mix[0]), pool_scale, hgrn_lb, hgrn_norm,
               _bf(w_pool_up[0]), _bf(w_hgrn_up[0]), _bf(w_out[0]))
    ffn_w = (g_ffn, _bf(w_ffn_gate[0]), _bf(w_ffn_up[0]), _bf(w_ffn_down[0]), g_ple,
             _bf(w_ple_gate[0]), _bf(w_ple_proj[0]), g_final.reshape(1, d_model))

    x1_p, pool_p, hgrn_p = _mixer_prompt(x_prompt, mixer_w, tile=512, chunk=128)
    x1_s, pool_s, hgrn_s = _mixer_sample(x_sample, state_pool[0], state_hgrn[0], mixer_w, tile_b=16)

    y_p = _ffn_ple(x1_p.reshape(bsz * seq, d_model), p_prompt[0].reshape(bsz * seq, -1), ffn_w, tile=512)
    y_s = _ffn_ple(x1_s, p_sample[0].reshape(dbsz * dseq, -1), ffn_w, tile=512)

    return (y_p.reshape(bsz, seq, d_model), y_s.reshape(dbsz, dseq, d_model),
            pool_p[None, :, POOL_HALO - buf:, :], hgrn_p[None],
            pool_s[None, :, POOL_HALO - buf:, :], hgrn_s[None])
```

```python
import functools

import jax
import jax.numpy as jnp
from jax import lax
from jax.experimental import pallas as pl
from jax.experimental.pallas import tpu as pltpu

F32 = jnp.float32
BF16 = jnp.bfloat16

EPS = 1e-6
POOL_WINDOWS = (2, 4, 8, 16)
POOL_HALO = 16
SUB = 16
LANES = 128
SUBLANES = 8
VMEM_LIMIT = 56 * 1024 * 1024

_NT = (((1,), (1,)), ((), ()))
_TN = (((0,), (0,)), ((), ()))


def _bf(x):
    return x.astype(BF16)


def _dot(a, b):
    return jnp.dot(a, b, preferred_element_type=F32)


def _rms(x, gain):
    ms = jnp.mean(x * x, axis=-1, keepdims=True)
    return x * lax.rsqrt(ms + EPS) * gain


def _silu(x):
    return x * jax.nn.sigmoid(x)


def _forget_lower_bound(lb_ref):
    a0 = lb_ref[0:1, :]
    a1 = lb_ref[1:2, :]
    m = jnp.maximum(a0, a1)
    e0 = jnp.exp(a0 - m)
    e1 = jnp.exp(a1 - m)
    return e0 / (e0 + e1)


def _scan_rows8(x, pos8):
    y = x.reshape(x.shape[0] // SUBLANES, SUBLANES, x.shape[1])
    s = 1
    while s < SUBLANES:
        y = y + jnp.where(pos8 >= s, pltpu.roll(y, s, axis=1), 0.0)
        s *= 2
    return y


def _last_row(y):
    return jnp.broadcast_to(y[..., SUBLANES - 1:SUBLANES, :], y.shape)


def _finish_head(a, v, s_prev, q_big, k_big, dec_row):
    vb = _bf(v)
    if a.shape[1] % LANES == 0:
        o = _dot(jnp.concatenate([_bf(a), _bf(q_big)], axis=1),
                 jnp.concatenate([vb, _bf(s_prev)], axis=0))
    else:
        o = _dot(_bf(a), vb) + _dot(_bf(q_big), _bf(s_prev))
    ds = lax.dot_general(_bf(k_big), vb, _TN, preferred_element_type=F32)
    dk = s_prev.shape[0]
    dec = jnp.transpose(jnp.broadcast_to(dec_row, (dk, dk)))
    return o, dec * s_prev + ds


def _scale_blocks(x, factors):
    out = []
    for i, f in enumerate(factors):
        blk = x[i * SUB:(i + 1) * SUB]
        out.append(blk if f is None else blk * jnp.concatenate([f] * (SUB // SUBLANES), axis=0))
    return jnp.concatenate(out, axis=0)


def _chunk_head(q, k, logf, v, s_prev, pos8, masks):
    n = q.shape[0]
    nb = n // SUB
    y = _scan_rows8(logf, pos8)
    cs_blk, tots = [], []
    for i in range(nb):
        lo = y[2 * i]
        hi = y[2 * i + 1] + _last_row(y[2 * i])
        cs_blk += [lo, hi]
        tots.append(_last_row(hi))
    cs = jnp.concatenate(cs_blk, axis=0)
    rs = jnp.concatenate([t for t in tots for _ in range(SUB // SUBLANES)], axis=0) - cs
    ecs = jnp.exp(cs)
    qe = q * ecs
    ke = k * jnp.exp(-cs)
    kd = k * jnp.exp(rs)

    def span(lo, hi):
        acc = None
        for m in range(lo, hi):
            acc = tots[m] if acc is None else acc + tots[m]
        return acc

    def expo(t):
        return None if t is None else jnp.exp(t)

    a01 = lax.dot_general(_bf(qe), _bf(jnp.concatenate([ke, kd], axis=0)), _NT,
                          preferred_element_type=F32)
    a = jnp.where(masks[0], a01[:, :n], jnp.where(masks[1], a01[:, n:], 0.0))
    lvl = 2
    w = 2
    while w * SUB < n:
        qf = [expo(span((i // w) * w, i)) if (i // w) % 2 == 1 else None for i in range(nb)]
        kf = [expo(span(i + 1, (i // w + 1) * w)) if (i // w) % 2 == 0 else None for i in range(nb)]
        a_l = lax.dot_general(_bf(_scale_blocks(qe, qf)), _bf(_scale_blocks(kd, kf)), _NT,
                              preferred_element_type=F32)
        a = jnp.where(masks[lvl], a_l, a)
        lvl += 1
        w *= 2
    q_big = _scale_blocks(qe, [expo(span(0, i)) for i in range(nb)])
    k_big = _scale_blocks(kd, [expo(span(i + 1, nb)) for i in range(nb)])
    dec_row = jnp.exp(span(0, nb))[0:1, :]
    return _finish_head(a, v, s_prev, q_big, k_big, dec_row)


def _gated_merge_out(x, h, merged_a, o_raw, zg, w_in_ref, hgrn_norm_ref, w_hgrn_up_ref, w_out_ref,
                     col_gb, n_heads):
    pieces = []
    for hh in range(n_heads):
        ln = slice(hh * LANES, (hh + 1) * LANES)
        oh = o_raw[:, ln]
        oh = oh * lax.rsqrt(jnp.mean(oh * oh, axis=-1, keepdims=True) + EPS)
        pieces.append(oh * hgrn_norm_ref[:, ln])
    o = jnp.concatenate(pieces, axis=-1) * _silu(zg)
    yb = _dot(_bf(o), w_hgrn_up_ref[...])
    gb = _dot(h, w_in_ref[:, col_gb:col_gb + x.shape[1]])
    merged = merged_a + jax.nn.sigmoid(gb) * yb
    return x + _dot(_bf(merged), w_out_ref[...])


def _pool_branch(h, u, shifted, inv_cnt, w_in_ref, w_pool_mix_ref, pool_scale_ref, w_pool_up_ref,
                 col_ga, d_model):
    pieces = []
    for g, w in enumerate(POOL_WINDOWS):
        ln = slice(g * LANES, (g + 1) * LANES)
        ug = u[:, ln]
        acc = ug
        for j in range(1, w):
            acc = acc + shifted(j, ln)
        pooled = acc * inv_cnt(w) - ug
        pieces.append(_dot(_bf(pooled), w_pool_mix_ref[g]))
    pool_out = jnp.concatenate(pieces, axis=-1) * pool_scale_ref[...]
    ya = _dot(_bf(pool_out), w_pool_up_ref[...])
    ga = _dot(h, w_in_ref[:, col_ga:col_ga + d_model])
    return jax.nn.sigmoid(ga) * ya


def _mixer_prompt_kernel(x_ref, g_mix_ref, w_in_ref, w_pool_mix_ref, pool_scale_ref, lb_ref,
                         hgrn_norm_ref, w_pool_up_ref, w_hgrn_up_ref, w_out_ref,
                         x1_ref, pool_out_ref, hgrn_out_ref,
                         u_scr, z_scr, o_scr, s_scr, *, chunk):
    tm, d_model = x_ref.shape
    pw = u_scr.shape[1]
    hw = o_scr.shape[1]
    n_heads = hw // LANES
    t = pl.program_id(1)

    @pl.when(t == 0)
    def _():
        u_scr[0:POOL_HALO, :] = jnp.zeros((POOL_HALO, pw), F32)
        s_scr[...] = jnp.zeros(s_scr.shape, F32)

    x = x_ref[...]
    h = _bf(_rms(x, g_mix_ref[...]))

    u = _dot(h, w_in_ref[:, 0:pw])
    u_scr[POOL_HALO:POOL_HALO + tm, :] = u
    seen = lax.broadcasted_iota(jnp.int32, (tm, 1), 0) + (t * tm + 1)

    def shifted(j, ln):
        return u_scr[pl.ds(POOL_HALO - j, tm), ln]

    def inv_cnt(w):
        return 1.0 / jnp.minimum(seen, w).astype(F32)

    col_ga = pw + 4 * hw
    merged_a = _pool_branch(h, u, shifted, inv_cnt, w_in_ref, w_pool_mix_ref, pool_scale_ref,
                            w_pool_up_ref, col_ga, d_model)
    u_scr[0:POOL_HALO, :] = u_scr[tm:tm + POOL_HALO, :]

    z_scr[...] = _dot(h, w_in_ref[:, pw:pw + 4 * hw])
    lb = _forget_lower_bound(lb_ref)

    pos8 = lax.broadcasted_iota(jnp.int32, (1, SUBLANES, LANES), 1)
    ti = lax.broadcasted_iota(jnp.int32, (chunk, chunk), 0)
    si = lax.broadcasted_iota(jnp.int32, (chunk, chunk), 1)
    masks = [((ti // SUB) == (si // SUB)) & (si <= ti)]
    b = SUB
    while b < chunk:
        masks.append(((ti // (2 * b)) == (si // (2 * b))) & ((ti & b) != 0) & ((si & b) == 0))
        b *= 2

    for c in range(tm // chunk):
        rows = slice(c * chunk, (c + 1) * chunk)
        for hh in range(n_heads):
            ln = slice(hh * LANES, (hh + 1) * LANES)
            zq = z_scr[rows, hh * LANES:(hh + 1) * LANES]
            zf = z_scr[rows, hw + hh * LANES:hw + (hh + 1) * LANES]
            v = z_scr[rows, 2 * hw + hh * LANES:2 * hw + (hh + 1) * LANES]
            lbh = lb[:, ln]
            fg = lbh + (1.0 - lbh) * jax.nn.sigmoid(zf)
            o, s_new = _chunk_head(_silu(zq), 1.0 - fg, jnp.log(fg), v, s_scr[hh], pos8, masks)
            s_scr[hh] = s_new
            o_scr[rows, ln] = o

    x1_ref[...] = _gated_merge_out(x, h, merged_a, o_scr[...], z_scr[:, 3 * hw:4 * hw], w_in_ref,
                                   hgrn_norm_ref, w_hgrn_up_ref, w_out_ref, col_ga + d_model, n_heads)

    @pl.when(t == pl.num_programs(1) - 1)
    def _():
        pool_out_ref[...] = u_scr[0:POOL_HALO, :]
        hgrn_out_ref[...] = s_scr[...]


def _mixer_sample_kernel(x_ref, sp_ref, sh_ref, g_mix_ref, w_in_ref, w_pool_mix_ref, pool_scale_ref,
                         lb_ref, hgrn_norm_ref, w_pool_up_ref, w_hgrn_up_ref, w_out_ref,
                         x1_ref, pool_out_ref, hgrn_out_ref,
                         e_scr, z_scr, o_scr, qe_scr, ke_scr, kd_scr, dec_scr, *, seq):
    rows_n, d_model = x_ref.shape
    tb, buf, pw = sp_ref.shape
    hw = o_scr.shape[1]
    n_heads = hw // LANES
    ext = e_scr.shape[1]

    x = x_ref[...]
    h = _bf(_rms(x, g_mix_ref[...]))

    u = _dot(h, w_in_ref[:, 0:pw])
    e_scr[:, ext - seq - buf:ext - seq, :] = sp_ref[...]
    e_scr[:, ext - seq:ext, :] = u.reshape(tb, seq, pw)

    def shifted(j, ln):
        return e_scr[:, pl.ds(ext - seq - j, seq), ln].reshape(rows_n, LANES)

    def inv_cnt(w):
        return 1.0 / w

    col_ga = pw + 4 * hw
    merged_a = _pool_branch(h, u, shifted, inv_cnt, w_in_ref, w_pool_mix_ref, pool_scale_ref,
                            w_pool_up_ref, col_ga, d_model)
    pool_out_ref[...] = e_scr[:, ext - POOL_HALO:ext, :]

    z_scr[...] = _dot(h, w_in_ref[:, pw:pw + 4 * hw])
    lb = _forget_lower_bound(lb_ref)
    pos8 = lax.broadcasted_iota(jnp.int32, (1, SUBLANES, LANES), 1)
    for hh in range(n_heads):
        ln = slice(hh * LANES, (hh + 1) * LANES)
        zq = z_scr[:, hh * LANES:(hh + 1) * LANES]
        zf = z_scr[:, hw + hh * LANES:hw + (hh + 1) * LANES]
        lbh = lb[:, ln]
        fg = lbh + (1.0 - lbh) * jax.nn.sigmoid(zf)
        k = 1.0 - fg
        q = _silu(zq)
        y = _scan_rows8(jnp.log(fg), pos8)
        tot = _last_row(y)
        cs = y.reshape(rows_n, LANES)
        qe_scr[:, ln] = q * jnp.exp(cs)
        ke_scr[:, ln] = k * jnp.exp(-cs)
        kd_scr[:, ln] = k * jnp.exp((tot - y).reshape(rows_n, LANES))
        dec_scr[:, ln] = jnp.exp(tot).reshape(rows_n, LANES)

    ti = lax.broadcasted_iota(jnp.int32, (seq, seq), 0)
    si = lax.broadcasted_iota(jnp.int32, (seq, seq), 1)
    causal = si <= ti

    def seq_body(b, carry):
        r0 = pl.multiple_of(b * seq, seq)
        rows = pl.ds(r0, seq)
        for hh in range(n_heads):
            ln = slice(hh * LANES, (hh + 1) * LANES)
            qe = qe_scr[rows, ln]
            v = z_scr[rows, 2 * hw + hh * LANES:2 * hw + (hh + 1) * LANES]
            a = lax.dot_general(_bf(qe), _bf(ke_scr[rows, ln]), _NT, preferred_element_type=F32)
            o, s_new = _finish_head(jnp.where(causal, a, 0.0), v, sh_ref[b, hh], qe,
                                    kd_scr[rows, ln], dec_scr[pl.ds(r0, 1), ln])
            hgrn_out_ref[b, hh] = s_new
            o_scr[rows, ln] = o
        return carry

    lax.fori_loop(0, tb, seq_body, 0, unroll=4)

    x1_ref[...] = _gated_merge_out(x, h, merged_a, o_scr[...], z_scr[:, 3 * hw:4 * hw], w_in_ref,
                                   hgrn_norm_ref, w_hgrn_up_ref, w_out_ref, col_ga + d_model, n_heads)


def _ffn_ple_kernel(x_ref, p_ref, g_ffn_ref, w_gate_ref, w_up_ref, w_down_ref, g_ple_ref,
                    w_ple_gate_ref, w_ple_proj_ref, g_final_ref, y_ref):
    x = x_ref[...]
    h2 = _bf(_rms(x, g_ffn_ref[...]))
    act = _silu(_dot(h2, w_gate_ref[...])) * _dot(h2, w_up_ref[...])
    x = x + _dot(_bf(act), w_down_ref[...])
    h3 = _bf(_rms(x, g_ple_ref[...]))
    gate = jax.nn.sigmoid(_dot(h3, w_ple_gate_ref[...]))
    emb = _dot(_bf(p_ref[...]), w_ple_proj_ref[...])
    x = x + gate * emb
    y_ref[...] = _rms(x, g_final_ref[...])


def _whole(_):
    return pl.BlockSpec(memory_space=pltpu.VMEM)


def _mixer_prompt(x, weights, *, tile, chunk):
    bsz, seq, d_model = x.shape
    pw = weights[3].shape[1]
    hw = weights[5].shape[1]
    n_heads = hw // LANES
    assert seq % tile == 0 and tile % chunk == 0 and chunk % SUB == 0
    return pl.pallas_call(
        functools.partial(_mixer_prompt_kernel, chunk=chunk),
        grid=(bsz, seq // tile),
        in_specs=[pl.BlockSpec((None, tile, d_model), lambda b, t: (b, t, 0))] + [_whole(w) for w in weights],
        out_specs=[
            pl.BlockSpec((None, tile, d_model), lambda b, t: (b, t, 0)),
            pl.BlockSpec((None, POOL_HALO, pw), lambda b, t: (b, 0, 0)),
            pl.BlockSpec((None, n_heads, LANES, LANES), lambda b, t: (b, 0, 0, 0)),
        ],
        out_shape=[
            jax.ShapeDtypeStruct((bsz, seq, d_model), F32),
            jax.ShapeDtypeStruct((bsz, POOL_HALO, pw), F32),
            jax.ShapeDtypeStruct((bsz, n_heads, LANES, LANES), F32),
        ],
        scratch_shapes=[
            pltpu.VMEM((POOL_HALO + tile, pw), F32),
            pltpu.VMEM((tile, 4 * hw), F32),
            pltpu.VMEM((tile, hw), F32),
            pltpu.VMEM((n_heads, LANES, LANES), F32),
        ],
        compiler_params=pltpu.CompilerParams(
            dimension_semantics=("parallel", "arbitrary"), vmem_limit_bytes=VMEM_LIMIT),
        name="mixer_prompt",
    )(x, *weights)


def _mixer_sample(x, state_pool, state_hgrn, weights, *, tile_b):
    bsz, seq, d_model = x.shape
    _, buf, pw = state_pool.shape
    _, n_heads, dk, dv = state_hgrn.shape
    hw = n_heads * dv
    assert bsz % tile_b == 0 and seq == 8 and buf + 1 == POOL_HALO and dk == LANES and dv == LANES
    rows = tile_b * seq
    xf = x.reshape(bsz * seq, d_model)
    return pl.pallas_call(
        functools.partial(_mixer_sample_kernel, seq=seq),
        grid=(bsz // tile_b,),
        in_specs=[
            pl.BlockSpec((rows, d_model), lambda i: (i, 0)),
            pl.BlockSpec((tile_b, buf, pw), lambda i: (i, 0, 0)),
            pl.BlockSpec((tile_b, n_heads, dk, dv), lambda i: (i, 0, 0, 0)),
        ] + [_whole(w) for w in weights],
        out_specs=[
            pl.BlockSpec((rows, d_model), lambda i: (i, 0)),
            pl.BlockSpec((tile_b, POOL_HALO, pw), lambda i: (i, 0, 0)),
            pl.BlockSpec((tile_b, n_heads, dk, dv), lambda i: (i, 0, 0, 0)),
        ],
        out_shape=[
            jax.ShapeDtypeStruct((bsz * seq, d_model), F32),
            jax.ShapeDtypeStruct((bsz, POOL_HALO, pw), F32),
            jax.ShapeDtypeStruct((bsz, n_heads, dk, dv), F32),
        ],
        scratch_shapes=[
            pltpu.VMEM((tile_b, 1 + buf + seq, pw), F32),
            pltpu.VMEM((rows, 4 * hw), F32),
            pltpu.VMEM((rows, hw), F32),
            pltpu.VMEM((rows, hw), F32),
            pltpu.VMEM((rows, hw), F32),
            pltpu.VMEM((rows, hw), F32),
            pltpu.VMEM((rows, hw), F32),
        ],
        compiler_params=pltpu.CompilerParams(
            dimension_semantics=("parallel",), vmem_limit_bytes=VMEM_LIMIT),
        name="mixer_sample",
    )(xf, state_pool, state_hgrn, *weights)


def _ffn_ple(x, p, weights, *, tile):
    n, d_model = x.shape
    p_dim = p.shape[1]
    assert n % tile == 0
    return pl.pallas_call(
        _ffn_ple_kernel,
        grid=(n // tile,),
        in_specs=[
            pl.BlockSpec((tile, d_model), lambda i: (i, 0)),
            pl.BlockSpec((tile, p_dim), lambda i: (i, 0)),
        ] + [_whole(w) for w in weights],
        out_specs=pl.BlockSpec((tile, d_model), lambda i: (i, 0)),
        out_shape=jax.ShapeDtypeStruct((n, d_model), F32),
        compiler_params=pltpu.CompilerParams(
            dimension_semantics=("parallel",), vmem_limit_bytes=VMEM_LIMIT),
        name="ffn_ple",
    )(x, p, *weights)


def kernel(x_prompt, x_sample, state_pool, state_hgrn, p_prompt, p_sample, g_mix, w_in, w_pool_mix, pool_scale, hgrn_lb, hgrn_norm, w_pool_up, w_hgrn_up, w_out, g_ffn, w_ffn_gate, w_ffn_up, w_ffn_down, g_ple, w_ple_gate, w_ple_proj, g_final):
    depth = w_in.shape[0]
    assert depth == 1 and hgrn_lb.shape[0] == 2
    bsz, seq, d_model = x_prompt.shape
    dbsz, dseq, _ = x_sample.shape
    buf = state_pool.shape[2]

    mixer_w = (g_mix, _bf(w_in[0]), _bf(w_pool_mix[0]), pool_scale, hgrn_lb, hgrn_norm,
               _bf(w_pool_up[0]), _bf(w_hgrn_up[0]), _bf(w_out[0]))
    ffn_w = (g_ffn, _bf(w_ffn_gate[0]), _bf(w_ffn_up[0]), _bf(w_ffn_down[0]), g_ple,
             _bf(w_ple_gate[0]), _bf(w_ple_proj[0]), g_final.reshape(1, d_model))

    x1_p, pool_p, hgrn_p = _mixer_prompt(x_prompt, mixer_w, tile=512, chunk=128)
    x1_s, pool_s, hgrn_s = _mixer_sample(x_sample, state_pool[0], state_hgrn[0], mixer_w, tile_b=16)

    y_p = _ffn_ple(x1_p.reshape(bsz * seq, d_model), p_prompt[0].reshape(bsz * seq, -1), ffn_w, tile=512)
    y_s = _ffn_ple(x1_s, p_sample[0].reshape(dbsz * dseq, -1), ffn_w, tile=512)

    return (y_p.reshape(bsz, seq, d_model), y_s.reshape(dbsz, dseq, d_model),
            pool_p[None, :, POOL_HALO - buf:, :], hgrn_p[None],
            pool_s[None, :, POOL_HALO - buf:, :], hgrn_s[None])
```

```python
import functools

import jax
import jax.numpy as jnp
from jax import lax
from jax.experimental import pallas as pl
from jax.experimental.pallas import tpu as pltpu

F32 = jnp.float32
BF16 = jnp.bfloat16

EPS = 1e-6
POOL_WINDOWS = (2, 4, 8, 16)
POOL_HALO = 16
SUB = 16
LANES = 128
SUBLANES = 8
VMEM_LIMIT = 56 * 1024 * 1024
GATE_PIECE = 256

_NT = (((1,), (1,)), ((), ()))
_TN = (((0,), (0,)), ((), ()))


def _bf(x):
    return x.astype(BF16)


def _dot(a, b):
    return jnp.dot(a, b, preferred_element_type=F32)


def _rms(x, gain):
    ms = jnp.mean(x * x, axis=-1, keepdims=True)
    return x * lax.rsqrt(ms + EPS) * gain


def _silu(x):
    return x * jax.nn.sigmoid(x)


def _forget_lower_bound(lb_ref):
    a0 = lb_ref[0:1, :]
    a1 = lb_ref[1:2, :]
    m = jnp.maximum(a0, a1)
    e0 = jnp.exp(a0 - m)
    e1 = jnp.exp(a1 - m)
    return e0 / (e0 + e1)


def _scan_rows8(x, pos8):
    y = x.reshape(x.shape[0] // SUBLANES, SUBLANES, x.shape[1])
    s = 1
    while s < SUBLANES:
        y = y + jnp.where(pos8 >= s, pltpu.roll(y, s, axis=1), 0.0)
        s *= 2
    return y


def _last_row(y):
    return jnp.broadcast_to(y[..., SUBLANES - 1:SUBLANES, :], y.shape)


def _finish_head(a, v, s_prev, q_big, k_big, dec_row):
    vb = _bf(v)
    if a.shape[1] % LANES == 0:
        o = _dot(jnp.concatenate([_bf(a), _bf(q_big)], axis=1),
                 jnp.concatenate([vb, _bf(s_prev)], axis=0))
    else:
        o = _dot(_bf(a), vb) + _dot(_bf(q_big), _bf(s_prev))
    ds = lax.dot_general(_bf(k_big), vb, _TN, preferred_element_type=F32)
    dk = s_prev.shape[0]
    dec = jnp.transpose(jnp.broadcast_to(dec_row, (dk, dk)))
    return o, dec * s_prev + ds


def _scale_blocks(x, factors):
    out = []
    for i, f in enumerate(factors):
        blk = x[i * SUB:(i + 1) * SUB]
        out.append(blk if f is None else blk * jnp.concatenate([f] * (SUB // SUBLANES), axis=0))
    return jnp.concatenate(out, axis=0)


def _chunk_head(q, k, logf, v, s_prev, pos8, masks):
    n = q.shape[0]
    nb = n // SUB
    y = _scan_rows8(logf, pos8)
    cs_blk, tots = [], []
    for i in range(nb):
        lo = y[2 * i]
        hi = y[2 * i + 1] + _last_row(y[2 * i])
        cs_blk += [lo, hi]
        tots.append(_last_row(hi))
    cs = jnp.concatenate(cs_blk, axis=0)
    rs = jnp.concatenate([t for t in tots for _ in range(SUB // SUBLANES)], axis=0) - cs
    ecs = jnp.exp(cs)
    qe = q * ecs
    ke = k * jnp.exp(-cs)
    kd = k * jnp.exp(rs)

    def span(lo, hi):
        acc = None
        for m in range(lo, hi):
            acc = tots[m] if acc is None else acc + tots[m]
        return acc

    def expo(t):
        return None if t is None else jnp.exp(t)

    a01 = lax.dot_general(_bf(qe), _bf(jnp.concatenate([ke, kd], axis=0)), _NT,
                          preferred_element_type=F32)
    a = jnp.where(masks[0], a01[:, :n], jnp.where(masks[1], a01[:, n:], 0.0))
    lvl = 2
    w = 2
    while w * SUB < n:
        qf = [expo(span((i // w) * w, i)) if (i // w) % 2 == 1 else None for i in range(nb)]
        kf = [expo(span(i + 1, (i // w + 1) * w)) if (i // w) % 2 == 0 else None for i in range(nb)]
        a_l = lax.dot_general(_bf(_scale_blocks(qe, qf)), _bf(_scale_blocks(kd, kf)), _NT,
                              preferred_element_type=F32)
        a = jnp.where(masks[lvl], a_l, a)
        lvl += 1
        w *= 2
    q_big = _scale_blocks(qe, [expo(span(0, i)) for i in range(nb)])
    k_big = _scale_blocks(kd, [expo(span(i + 1, nb)) for i in range(nb)])
    dec_row = jnp.exp(span(0, nb))[0:1, :]
    return _finish_head(a, v, s_prev, q_big, k_big, dec_row)


def _gated_merge_out(x, ya, ga, gb, o_raw, zg, hgrn_norm_ref, w_hgrn_up_ref, w_out_ref, n_heads):
    pieces = []
    for hh in range(n_heads):
        ln = slice(hh * LANES, (hh + 1) * LANES)
        oh = o_raw[:, ln]
        oh = oh * lax.rsqrt(jnp.mean(oh * oh, axis=-1, keepdims=True) + EPS)
        pieces.append(oh * hgrn_norm_ref[:, ln])
    o = jnp.concatenate(pieces, axis=-1) * _silu(zg)
    yb = _dot(_bf(o), w_hgrn_up_ref[...])
    merged = jax.nn.sigmoid(ga) * ya + jax.nn.sigmoid(gb) * yb
    return x + _dot(_bf(merged), w_out_ref[...])


def _pool_branch(u, shifted, inv_cnt, w_pool_mix_ref, pool_scale_ref, w_pool_up_ref):
    pieces = []
    for g, w in enumerate(POOL_WINDOWS):
        ln = slice(g * LANES, (g + 1) * LANES)
        ug = u[:, ln]
        acc = ug
        for j in range(1, w):
            acc = acc + shifted(j, ln)
        pooled = acc * inv_cnt(w) - ug
        pieces.append(_dot(_bf(pooled), w_pool_mix_ref[g]))
    pool_out = jnp.concatenate(pieces, axis=-1) * pool_scale_ref[...]
    return _dot(_bf(pool_out), w_pool_up_ref[...])


def _mixer_prompt_kernel(x_ref, g_mix_ref, w_in_ref, w_pool_mix_ref, pool_scale_ref, lb_ref,
                         hgrn_norm_ref, w_pool_up_ref, w_hgrn_up_ref, w_out_ref,
                         x1_ref, pool_out_ref, hgrn_out_ref,
                         u_scr, z_scr, o_scr, s_scr, *, chunk):
    tm, d_model = x_ref.shape
    pw = u_scr.shape[1]
    hw = o_scr.shape[1]
    n_heads = hw // LANES
    t = pl.program_id(1)

    @pl.when(t == 0)
    def _():
        u_scr[0:POOL_HALO, :] = jnp.zeros((POOL_HALO, pw), F32)
        s_scr[...] = jnp.zeros(s_scr.shape, F32)

    x = x_ref[...]
    h = _bf(_rms(x, g_mix_ref[...]))

    u = _dot(h, w_in_ref[:, 0:pw])
    u_scr[POOL_HALO:POOL_HALO + tm, :] = u
    z_scr[...] = _dot(h, w_in_ref[:, pw:pw + 4 * hw])
    seen = lax.broadcasted_iota(jnp.int32, (tm, 1), 0) + (t * tm + 1)

    def shifted(j, ln):
        return u_scr[pl.ds(POOL_HALO - j, tm), ln]

    def inv_cnt(w):
        return 1.0 / jnp.minimum(seen, w).astype(F32)

    col_ga = pw + 4 * hw
    ya = _pool_branch(u, shifted, inv_cnt, w_pool_mix_ref, pool_scale_ref, w_pool_up_ref)
    u_scr[0:POOL_HALO, :] = u_scr[tm:tm + POOL_HALO, :]

    lb = _forget_lower_bound(lb_ref)

    pos8 = lax.broadcasted_iota(jnp.int32, (1, SUBLANES, LANES), 1)
    ti = lax.broadcasted_iota(jnp.int32, (chunk, chunk), 0)
    si = lax.broadcasted_iota(jnp.int32, (chunk, chunk), 1)
    masks = [((ti // SUB) == (si // SUB)) & (si <= ti)]
    b = SUB
    while b < chunk:
        masks.append(((ti // (2 * b)) == (si // (2 * b))) & ((ti & b) != 0) & ((si & b) == 0))
        b *= 2

    gate_cols = [(col_ga + j * GATE_PIECE) for j in range(2 * d_model // GATE_PIECE)]
    gate_pieces = []
    steps = [(c, hh) for c in range(tm // chunk) for hh in range(n_heads)]
    for k, (c, hh) in enumerate(steps):
        rows = slice(c * chunk, (c + 1) * chunk)
        ln = slice(hh * LANES, (hh + 1) * LANES)
        zq = z_scr[rows, hh * LANES:(hh + 1) * LANES]
        zf = z_scr[rows, hw + hh * LANES:hw + (hh + 1) * LANES]
        v = z_scr[rows, 2 * hw + hh * LANES:2 * hw + (hh + 1) * LANES]
        lbh = lb[:, ln]
        fg = lbh + (1.0 - lbh) * jax.nn.sigmoid(zf)
        o, s_new = _chunk_head(_silu(zq), 1.0 - fg, jnp.log(fg), v, s_scr[hh], pos8, masks)
        s_scr[hh] = s_new
        o_scr[rows, ln] = o
        while len(gate_pieces) < (k + 1) * len(gate_cols) // len(steps):
            c0 = gate_cols[len(gate_pieces)]
            gate_pieces.append(_dot(h, w_in_ref[:, c0:c0 + GATE_PIECE]))
    gates = jnp.concatenate(gate_pieces, axis=1)

    x1_ref[...] = _gated_merge_out(x, ya, gates[:, :d_model], gates[:, d_model:], o_scr[...],
                                   z_scr[:, 3 * hw:4 * hw], hgrn_norm_ref, w_hgrn_up_ref, w_out_ref,
                                   n_heads)

    @pl.when(t == pl.num_programs(1) - 1)
    def _():
        pool_out_ref[...] = u_scr[0:POOL_HALO, :]
        hgrn_out_ref[...] = s_scr[...]


def _mixer_sample_kernel(x_ref, sp_ref, sh_ref, g_mix_ref, w_in_ref, w_pool_mix_ref, pool_scale_ref,
                         lb_ref, hgrn_norm_ref, w_pool_up_ref, w_hgrn_up_ref, w_out_ref,
                         x1_ref, pool_out_ref, hgrn_out_ref,
                         e_scr, z_scr, o_scr, qe_scr, ke_scr, kd_scr, dec_scr, *, seq):
    rows_n, d_model = x_ref.shape
    tb, buf, pw = sp_ref.shape
    hw = o_scr.shape[1]
    n_heads = hw // LANES
    ext = e_scr.shape[1]

    x = x_ref[...]
    h = _bf(_rms(x, g_mix_ref[...]))

    u = _dot(h, w_in_ref[:, 0:pw])
    e_scr[:, ext - seq - buf:ext - seq, :] = sp_ref[...]
    e_scr[:, ext - seq:ext, :] = u.reshape(tb, seq, pw)

    def shifted(j, ln):
        return e_scr[:, pl.ds(ext - seq - j, seq), ln].reshape(rows_n, LANES)

    def inv_cnt(w):
        return 1.0 / w

    col_ga = pw + 4 * hw
    ya = _pool_branch(u, shifted, inv_cnt, w_pool_mix_ref, pool_scale_ref, w_pool_up_ref)
    pool_out_ref[...] = e_scr[:, ext - POOL_HALO:ext, :]

    z_scr[...] = _dot(h, w_in_ref[:, pw:pw + 4 * hw])
    lb = _forget_lower_bound(lb_ref)
    pos8 = lax.broadcasted_iota(jnp.int32, (1, SUBLANES, LANES), 1)
    for hh in range(n_heads):
        ln = slice(hh * LANES, (hh + 1) * LANES)
        zq = z_scr[:, hh * LANES:(hh + 1) * LANES]
        zf = z_scr[:, hw + hh * LANES:hw + (hh + 1) * LANES]
        lbh = lb[:, ln]
        fg = lbh + (1.0 - lbh) * jax.nn.sigmoid(zf)
        k = 1.0 - fg
        q = _silu(zq)
        y = _scan_rows8(jnp.log(fg), pos8)
        tot = _last_row(y)
        cs = y.reshape(rows_n, LANES)
        qe_scr[:, ln] = q * jnp.exp(cs)
        ke_scr[:, ln] = k * jnp.exp(-cs)
        kd_scr[:, ln] = k * jnp.exp((tot - y).reshape(rows_n, LANES))
        dec_scr[:, ln] = jnp.exp(tot).reshape(rows_n, LANES)

    ti = lax.broadcasted_iota(jnp.int32, (seq, seq), 0)
    si = lax.broadcasted_iota(jnp.int32, (seq, seq), 1)
    causal = si <= ti

    def seq_body(b, carry):
        r0 = pl.multiple_of(b * seq, seq)
        rows = pl.ds(r0, seq)
        for hh in range(n_heads):
            ln = slice(hh * LANES, (hh + 1) * LANES)
            qe = qe_scr[rows, ln]
            v = z_scr[rows, 2 * hw + hh * LANES:2 * hw + (hh + 1) * LANES]
            a = lax.dot_general(_bf(qe), _bf(ke_scr[rows, ln]), _NT, preferred_element_type=F32)
            o, s_new = _finish_head(jnp.where(causal, a, 0.0), v, sh_ref[b, hh], qe,
                                    kd_scr[rows, ln], dec_scr[pl.ds(r0, 1), ln])
            hgrn_out_ref[b, hh] = s_new
            o_scr[rows, ln] = o
        return carry

    lax.fori_loop(0, tb, seq_body, 0, unroll=4)

    gates = _dot(h, w_in_ref[:, col_ga:col_ga + 2 * d_model])
    x1_ref[...] = _gated_merge_out(x, ya, gates[:, :d_model], gates[:, d_model:], o_scr[...],
                                   z_scr[:, 3 * hw:4 * hw], hgrn_norm_ref, w_hgrn_up_ref, w_out_ref,
                                   n_heads)


def _ffn_ple_kernel(x_ref, p_ref, g_ffn_ref, w_gate_ref, w_up_ref, w_down_ref, g_ple_ref,
                    w_ple_gate_ref, w_ple_proj_ref, g_final_ref, y_ref):
    x = x_ref[...]
    h2 = _bf(_rms(x, g_ffn_ref[...]))
    act = _silu(_dot(h2, w_gate_ref[...])) * _dot(h2, w_up_ref[...])
    x = x + _dot(_bf(act), w_down_ref[...])
    h3 = _bf(_rms(x, g_ple_ref[...]))
    gate = jax.nn.sigmoid(_dot(h3, w_ple_gate_ref[...]))
    emb = _dot(_bf(p_ref[...]), w_ple_proj_ref[...])
    x = x + gate * emb
    y_ref[...] = _rms(x, g_final_ref[...])


def _whole(_):
    return pl.BlockSpec(memory_space=pltpu.VMEM)


def _mixer_prompt(x, weights, *, tile, chunk):
    bsz, seq, d_model = x.shape
    pw = weights[3].shape[1]
    hw = weights[5].shape[1]
    n_heads = hw // LANES
    assert seq % tile == 0 and tile % chunk == 0 and chunk % SUB == 0
    return pl.pallas_call(
        functools.partial(_mixer_prompt_kernel, chunk=chunk),
        grid=(bsz, seq // tile),
        in_specs=[pl.BlockSpec((None, tile, d_model), lambda b, t: (b, t, 0))] + [_whole(w) for w in weights],
        out_specs=[
            pl.BlockSpec((None, tile, d_model), lambda b, t: (b, t, 0)),
            pl.BlockSpec((None, POOL_HALO, pw), lambda b, t: (b, 0, 0)),
            pl.BlockSpec((None, n_heads, LANES, LANES), lambda b, t: (b, 0, 0, 0)),
        ],
        out_shape=[
            jax.ShapeDtypeStruct((bsz, seq, d_model), F32),
            jax.ShapeDtypeStruct((bsz, POOL_HALO, pw), F32),
            jax.ShapeDtypeStruct((bsz, n_heads, LANES, LANES), F32),
        ],
        scratch_shapes=[
            pltpu.VMEM((POOL_HALO + tile, pw), F32),
            pltpu.VMEM((tile, 4 * hw), F32),
            pltpu.VMEM((tile, hw), F32),
            pltpu.VMEM((n_heads, LANES, LANES), F32),
        ],
        compiler_params=pltpu.CompilerParams(
            dimension_semantics=("parallel", "arbitrary"), vmem_limit_bytes=VMEM_LIMIT),
        name="mixer_prompt",
    )(x, *weights)


def _mixer_sample(x, state_pool, state_hgrn, weights, *, tile_b):
    bsz, seq, d_model = x.shape
    _, buf, pw = state_pool.shape
    _, n_heads, dk, dv = state_hgrn.shape
    hw = n_heads * dv
    assert bsz % tile_b == 0 and seq == 8 and buf + 1 == POOL_HALO and dk == LANES and dv == LANES
    rows = tile_b * seq
    xf = x.reshape(bsz * seq, d_model)
    return pl.pallas_call(
        functools.partial(_mixer_sample_kernel, seq=seq),
        grid=(bsz // tile_b,),
        in_specs=[
            pl.BlockSpec((rows, d_model), lambda i: (i, 0)),
            pl.BlockSpec((tile_b, buf, pw), lambda i: (i, 0, 0)),
            pl.BlockSpec((tile_b, n_heads, dk, dv), lambda i: (i, 0, 0, 0)),
        ] + [_whole(w) for w in weights],
        out_specs=[
            pl.BlockSpec((rows, d_model), lambda i: (i, 0)),
            pl.BlockSpec((tile_b, POOL_HALO, pw), lambda i: (i, 0, 0)),
            pl.BlockSpec((tile_b, n_heads, dk, dv), lambda i: (i, 0, 0, 0)),
        ],
        out_shape=[
            jax.ShapeDtypeStruct((bsz * seq, d_model), F32),
            jax.ShapeDtypeStruct((bsz, POOL_HALO, pw), F32),
            jax.ShapeDtypeStruct((bsz, n_heads, dk, dv), F32),
        ],
        scratch_shapes=[
            pltpu.VMEM((tile_b, 1 + buf + seq, pw), F32),
            pltpu.VMEM((rows, 4 * hw), F32),
            pltpu.VMEM((rows, hw), F32),
            pltpu.VMEM((rows, hw), F32),
            pltpu.VMEM((rows, hw), F32),
            pltpu.VMEM((rows, hw), F32),
            pltpu.VMEM((rows, hw), F32),
        ],
        compiler_params=pltpu.CompilerParams(
            dimension_semantics=("parallel",), vmem_limit_bytes=VMEM_LIMIT),
        name="mixer_sample",
    )(xf, state_pool, state_hgrn, *weights)


def _ffn_ple(x, p, weights, *, tile):
    n, d_model = x.shape
    p_dim = p.shape[1]
    assert n % tile == 0
    return pl.pallas_call(
        _ffn_ple_kernel,
        grid=(n // tile,),
        in_specs=[
            pl.BlockSpec((tile, d_model), lambda i: (i, 0)),
            pl.BlockSpec((tile, p_dim), lambda i: (i, 0)),
        ] + [_whole(w) for w in weights],
        out_specs=pl.BlockSpec((tile, d_model), lambda i: (i, 0)),
        out_shape=jax.ShapeDtypeStruct((n, d_model), F32),
        compiler_params=pltpu.CompilerParams(
            dimension_semantics=("parallel",), vmem_limit_bytes=VMEM_LIMIT),
        name="ffn_ple",
    )(x, p, *weights)


def kernel(x_prompt, x_sample, state_pool, state_hgrn, p_prompt, p_sample, g_mix, w_in, w_pool_mix, pool_scale, hgrn_lb, hgrn_norm, w_pool_up, w_hgrn_up, w_out, g_ffn, w_ffn_gate, w_ffn_up, w_ffn_down, g_ple, w_ple_gate, w_ple_proj, g_final):
    depth = w_in.shape[0]
    assert depth == 1 and hgrn_lb.shape[0] == 2
    bsz, seq, d_model = x_prompt.shape
    dbsz, dseq, _ = x_sample.shape
    buf = state_pool.shape[2]

    mixer_w = (g_mix, _bf(w_in[0]), _bf(w_pool_mix[0]), pool_scale, hgrn_lb, hgrn_norm,
               _bf(w_pool_up[0]), _bf(w_hgrn_up[0]), _bf(w_out[0]))
    ffn_w = (g_ffn, _bf(w_ffn_gate[0]), _bf(w_ffn_up[0]), _bf(w_ffn_down[0]), g_ple,
             _bf(w_ple_gate[0]), _bf(w_ple_proj[0]), g_final.reshape(1, d_model))

    x1_p, pool_p, hgrn_p = _mixer_prompt(x_prompt, mixer_w, tile=512, chunk=128)
    x1_s, pool_s, hgrn_s = _mixer_sample(x_sample, state_pool[0], state_hgrn[0], mixer_w, tile_b=16)

    y_p = _ffn_ple(x1_p.reshape(bsz * seq, d_model), p_prompt[0].reshape(bsz * seq, -1), ffn_w, tile=512)
    y_s = _ffn_ple(x1_s, p_sample[0].reshape(dbsz * dseq, -1), ffn_w, tile=512)

    return (y_p.reshape(bsz, seq, d_model), y_s.reshape(dbsz, dseq, d_model),
            pool_p[None, :, POOL_HALO - buf:, :], hgrn_p[None],
            pool_s[None, :, POOL_HALO - buf:, :], hgrn_s[None])
```

```python
import functools

import jax
import jax.numpy as jnp
from jax import lax
from jax.experimental import pallas as pl
from jax.experimental.pallas import tpu as pltpu

F32 = jnp.float32
BF16 = jnp.bfloat16

EPS = 1e-6
POOL_WINDOWS = (2, 4, 8, 16)
POOL_HALO = 16
SUB = 16
LANES = 128
SUBLANES = 8
VMEM_LIMIT = 56 * 1024 * 1024
GATE_PIECE = 256
PROMPT_STREAMS = 1
STREAM_LAG = 5

_NT = (((1,), (1,)), ((), ()))
_TN = (((0,), (0,)), ((), ()))


def _bf(x):
    return x.astype(BF16)


def _dot(a, b):
    return jnp.dot(a, b, preferred_element_type=F32)


def _rms(x, gain):
    ms = jnp.mean(x * x, axis=-1, keepdims=True)
    return x * lax.rsqrt(ms + EPS) * gain


def _silu(x):
    return x * jax.nn.sigmoid(x)


def _forget_lower_bound(lb_ref):
    a0 = lb_ref[0:1, :]
    a1 = lb_ref[1:2, :]
    m = jnp.maximum(a0, a1)
    e0 = jnp.exp(a0 - m)
    e1 = jnp.exp(a1 - m)
    return e0 / (e0 + e1)


def _scan_rows8(x, pos8):
    y = x.reshape(x.shape[0] // SUBLANES, SUBLANES, x.shape[1])
    s = 1
    while s < SUBLANES:
        y = y + jnp.where(pos8 >= s, pltpu.roll(y, s, axis=1), 0.0)
        s *= 2
    return y


def _last_row(y):
    return jnp.broadcast_to(y[..., SUBLANES - 1:SUBLANES, :], y.shape)


def _finish_head(a, v, s_prev, q_big, k_big, dec_row):
    vb = _bf(v)
    if a.shape[1] % LANES == 0:
        o = _dot(jnp.concatenate([_bf(a), _bf(q_big)], axis=1),
                 jnp.concatenate([vb, _bf(s_prev)], axis=0))
    else:
        o = _dot(_bf(a), vb) + _dot(_bf(q_big), _bf(s_prev))
    ds = lax.dot_general(_bf(k_big), vb, _TN, preferred_element_type=F32)
    dk = s_prev.shape[0]
    dec = jnp.transpose(jnp.broadcast_to(dec_row, (dk, dk)))
    return o, dec * s_prev + ds


def _scale_blocks(x, factors):
    out = []
    for i, f in enumerate(factors):
        blk = x[i * SUB:(i + 1) * SUB]
        out.append(blk if f is None else blk * jnp.concatenate([f] * (SUB // SUBLANES), axis=0))
    return jnp.concatenate(out, axis=0)


def _chunk_head(q, k, logf, v, s_prev, pos8, masks):
    n = q.shape[0]
    nb = n // SUB
    y = _scan_rows8(logf, pos8)
    cs_blk, tots = [], []
    for i in range(nb):
        lo = y[2 * i]
        hi = y[2 * i + 1] + _last_row(y[2 * i])
        cs_blk += [lo, hi]
        tots.append(_last_row(hi))
    cs = jnp.concatenate(cs_blk, axis=0)
    rs = jnp.concatenate([t for t in tots for _ in range(SUB // SUBLANES)], axis=0) - cs
    ecs = jnp.exp(cs)
    qe = q * ecs
    ke = k * jnp.exp(-cs)
    kd = k * jnp.exp(rs)

    def span(lo, hi):
        acc = None
        for m in range(lo, hi):
            acc = tots[m] if acc is None else acc + tots[m]
        return acc

    def expo(t):
        return None if t is None else jnp.exp(t)

    a01 = lax.dot_general(_bf(qe), _bf(jnp.concatenate([ke, kd], axis=0)), _NT,
                          preferred_element_type=F32)
    a = jnp.where(masks[0], a01[:, :n], jnp.where(masks[1], a01[:, n:], 0.0))
    lvl = 2
    w = 2
    while w * SUB < n:
        qf = [expo(span((i // w) * w, i)) if (i // w) % 2 == 1 else None for i in range(nb)]
        kf = [expo(span(i + 1, (i // w + 1) * w)) if (i // w) % 2 == 0 else None for i in range(nb)]
        a_l = lax.dot_general(_bf(_scale_blocks(qe, qf)), _bf(_scale_blocks(kd, kf)), _NT,
                              preferred_element_type=F32)
        a = jnp.where(masks[lvl], a_l, a)
        lvl += 1
        w *= 2
    q_big = _scale_blocks(qe, [expo(span(0, i)) for i in range(nb)])
    k_big = _scale_blocks(kd, [expo(span(i + 1, nb)) for i in range(nb)])
    dec_row = jnp.exp(span(0, nb))[0:1, :]
    return _finish_head(a, v, s_prev, q_big, k_big, dec_row)


def _gated_merge_out(x, ya, ga, gb, o_raw, zg, hgrn_norm_ref, w_hgrn_up_ref, w_out_ref, n_heads):
    pieces = []
    for hh in range(n_heads):
        ln = slice(hh * LANES, (hh + 1) * LANES)
        oh = o_raw[:, ln]
        oh = oh * lax.rsqrt(jnp.mean(oh * oh, axis=-1, keepdims=True) + EPS)
        pieces.append(oh * hgrn_norm_ref[:, ln])
    o = jnp.concatenate(pieces, axis=-1) * _silu(zg)
    yb = _dot(_bf(o), w_hgrn_up_ref[...])
    merged = jax.nn.sigmoid(ga) * ya + jax.nn.sigmoid(gb) * yb
    return x + _dot(_bf(merged), w_out_ref[...])


def _pool_branch(u, shifted, inv_cnt, w_pool_mix_ref, pool_scale_ref, w_pool_up_ref):
    pieces = []
    for g, w in enumerate(POOL_WINDOWS):
        ln = slice(g * LANES, (g + 1) * LANES)
        ug = u[:, ln]
        acc = ug
        for j in range(1, w):
            acc = acc + shifted(j, ln)
        pooled = acc * inv_cnt(w) - ug
        pieces.append(_dot(_bf(pooled), w_pool_mix_ref[g]))
    pool_out = jnp.concatenate(pieces, axis=-1) * pool_scale_ref[...]
    return _dot(_bf(pool_out), w_pool_up_ref[...])


def _prompt_stream(s, t, x_ref, g_mix_ref, w_in_ref, w_pool_mix_ref, pool_scale_ref, lb_ref,
                   hgrn_norm_ref, w_pool_up_ref, w_hgrn_up_ref, w_out_ref, x1_ref,
                   u_scr, z_scr, o_scr, s_scr, chunk, pos8, masks):
    _, tm, d_model = x_ref.shape
    pw = u_scr.shape[2]
    hw = o_scr.shape[2]
    n_heads = hw // LANES
    col_ga = pw + 4 * hw
    st = {"gates": []}

    def norm():
        st["h"] = _bf(_rms(x_ref[s], g_mix_ref[...]))
        st["lb"] = _forget_lower_bound(lb_ref)

    def proj_u():
        u_scr[s, POOL_HALO:POOL_HALO + tm, :] = _dot(st["h"], w_in_ref[:, 0:pw])

    def proj_z(j):
        def run():
            z_scr[s, :, j * hw:(j + 1) * hw] = _dot(st["h"], w_in_ref[:, pw + j * hw:pw + (j + 1) * hw])
        return run

    def pool():
        seen = lax.broadcasted_iota(jnp.int32, (tm, 1), 0) + (t * tm + 1)

        def shifted(j, ln):
            return u_scr[s, pl.ds(POOL_HALO - j, tm), ln]

        def inv_cnt(w):
            return 1.0 / jnp.minimum(seen, w).astype(F32)

        st["ya"] = _pool_branch(u_scr[s, POOL_HALO:POOL_HALO + tm, :], shifted, inv_cnt,
                                w_pool_mix_ref, pool_scale_ref, w_pool_up_ref)
        u_scr[s, 0:POOL_HALO, :] = u_scr[s, tm:tm + POOL_HALO, :]

    gate_cols = [(col_ga + j * GATE_PIECE) for j in range(2 * d_model // GATE_PIECE)]
    steps = [(c, hh) for c in range(tm // chunk) for hh in range(n_heads)]

    def step(k):
        c, hh = steps[k]

        def run():
            rows = slice(c * chunk, (c + 1) * chunk)
            ln = slice(hh * LANES, (hh + 1) * LANES)
            zq = z_scr[s, rows, hh * LANES:(hh + 1) * LANES]
            zf = z_scr[s, rows, hw + hh * LANES:hw + (hh + 1) * LANES]
            v = z_scr[s, rows, 2 * hw + hh * LANES:2 * hw + (hh + 1) * LANES]
            lbh = st["lb"][:, ln]
            fg = lbh + (1.0 - lbh) * jax.nn.sigmoid(zf)
            o, s_new = _chunk_head(_silu(zq), 1.0 - fg, jnp.log(fg), v, s_scr[s, hh], pos8, masks)
            s_scr[s, hh] = s_new
            o_scr[s, rows, ln] = o
            while len(st["gates"]) < (k + 1) * len(gate_cols) // len(steps):
                c0 = gate_cols[len(st["gates"])]
                st["gates"].append(_dot(st["h"], w_in_ref[:, c0:c0 + GATE_PIECE]))
        return run

    def tail():
        gates = jnp.concatenate(st["gates"], axis=1)
        x1_ref[s] = _gated_merge_out(x_ref[s], st["ya"], gates[:, :d_model], gates[:, d_model:],
                                     o_scr[s], z_scr[s, :, 3 * hw:4 * hw], hgrn_norm_ref,
                                     w_hgrn_up_ref, w_out_ref, n_heads)

    return ([norm, proj_u] + [proj_z(j) for j in range(4)] + [pool]
            + [step(k) for k in range(len(steps))] + [tail])


def _mixer_prompt_kernel(x_ref, g_mix_ref, w_in_ref, w_pool_mix_ref, pool_scale_ref, lb_ref,
                         hgrn_norm_ref, w_pool_up_ref, w_hgrn_up_ref, w_out_ref, *rest, chunk, n_cast):
    cast_in = rest[:n_cast]
    x1_ref, pool_out_ref, hgrn_out_ref = rest[n_cast:n_cast + 3]
    cast_out = rest[n_cast + 3:2 * n_cast + 3]
    u_scr, z_scr, o_scr, s_scr = rest[2 * n_cast + 3:]
    n_seq = x_ref.shape[0]
    t = pl.program_id(1)

    for src, dst in zip(cast_in, cast_out):
        dst[...] = _bf(src[...])

    @pl.when(t == 0)
    def _():
        u_scr[:, 0:POOL_HALO, :] = jnp.zeros((n_seq, POOL_HALO, u_scr.shape[2]), F32)
        s_scr[...] = jnp.zeros(s_scr.shape, F32)

    pos8 = lax.broadcasted_iota(jnp.int32, (1, SUBLANES, LANES), 1)
    ti = lax.broadcasted_iota(jnp.int32, (chunk, chunk), 0)
    si = lax.broadcasted_iota(jnp.int32, (chunk, chunk), 1)
    masks = [((ti // SUB) == (si // SUB)) & (si <= ti)]
    b = SUB
    while b < chunk:
        masks.append(((ti // (2 * b)) == (si // (2 * b))) & ((ti & b) != 0) & ((si & b) == 0))
        b *= 2

    streams = [_prompt_stream(s, t, x_ref, g_mix_ref, w_in_ref, w_pool_mix_ref, pool_scale_ref, lb_ref,
                              hgrn_norm_ref, w_pool_up_ref, w_hgrn_up_ref, w_out_ref, x1_ref,
                              u_scr, z_scr, o_scr, s_scr, chunk, pos8, masks) for s in range(n_seq)]
    for phases in streams:
        phases[0]()
    n = len(streams[0])
    for i in range(1, n + STREAM_LAG * (n_seq - 1)):
        for s, phases in enumerate(streams):
            if 1 <= i - s * STREAM_LAG < n:
                phases[i - s * STREAM_LAG]()

    @pl.when(t == pl.num_programs(1) - 1)
    def _():
        pool_out_ref[...] = u_scr[:, 0:POOL_HALO, :]
        hgrn_out_ref[...] = s_scr[...]


def _mixer_sample_kernel(x_ref, sp_ref, sh_ref, g_mix_ref, w_in_ref, w_pool_mix_ref, pool_scale_ref,
                         lb_ref, hgrn_norm_ref, w_pool_up_ref, w_hgrn_up_ref, w_out_ref,
                         x1_ref, pool_out_ref, hgrn_out_ref,
                         e_scr, z_scr, o_scr, qe_scr, ke_scr, kd_scr, dec_scr, *, seq):
    rows_n, d_model = x_ref.shape
    tb, buf, pw = sp_ref.shape
    hw = o_scr.shape[1]
    n_heads = hw // LANES
    ext = e_scr.shape[1]

    x = x_ref[...]
    h = _bf(_rms(x, g_mix_ref[...]))

    u = _dot(h, w_in_ref[:, 0:pw])
    e_scr[:, ext - seq - buf:ext - seq, :] = sp_ref[...]
    e_scr[:, ext - seq:ext, :] = u.reshape(tb, seq, pw)

    def shifted(j, ln):
        return e_scr[:, pl.ds(ext - seq - j, seq), ln].reshape(rows_n, LANES)

    def inv_cnt(w):
        return 1.0 / w

    col_ga = pw + 4 * hw
    ya = _pool_branch(u, shifted, inv_cnt, w_pool_mix_ref, pool_scale_ref, w_pool_up_ref)
    pool_out_ref[...] = e_scr[:, ext - POOL_HALO:ext, :]

    z_scr[...] = _dot(h, w_in_ref[:, pw:pw + 4 * hw])
    lb = _forget_lower_bound(lb_ref)
    pos8 = lax.broadcasted_iota(jnp.int32, (1, SUBLANES, LANES), 1)
    for hh in range(n_heads):
        ln = slice(hh * LANES, (hh + 1) * LANES)
        zq = z_scr[:, hh * LANES:(hh + 1) * LANES]
        zf = z_scr[:, hw + hh * LANES:hw + (hh + 1) * LANES]
        lbh = lb[:, ln]
        fg = lbh + (1.0 - lbh) * jax.nn.sigmoid(zf)
        k = 1.0 - fg
        q = _silu(zq)
        y = _scan_rows8(jnp.log(fg), pos8)
        tot = _last_row(y)
        cs = y.reshape(rows_n, LANES)
        qe_scr[:, ln] = q * jnp.exp(cs)
        ke_scr[:, ln] = k * jnp.exp(-cs)
        kd_scr[:, ln] = k * jnp.exp((tot - y).reshape(rows_n, LANES))
        dec_scr[:, ln] = jnp.exp(tot).reshape(rows_n, LANES)

    ti = lax.broadcasted_iota(jnp.int32, (seq, seq), 0)
    si = lax.broadcasted_iota(jnp.int32, (seq, seq), 1)
    causal = si <= ti

    def seq_body(b, carry):
        r0 = pl.multiple_of(b * seq, seq)
        rows = pl.ds(r0, seq)
        for hh in range(n_heads):
            ln = slice(hh * LANES, (hh + 1) * LANES)
            qe = qe_scr[rows, ln]
            v = z_scr[rows, 2 * hw + hh * LANES:2 * hw + (hh + 1) * LANES]
            a = lax.dot_general(_bf(qe), _bf(ke_scr[rows, ln]), _NT, preferred_element_type=F32)
            o, s_new = _finish_head(jnp.where(causal, a, 0.0), v, sh_ref[b, hh], qe,
                                    kd_scr[rows, ln], dec_scr[pl.ds(r0, 1), ln])
            hgrn_out_ref[b, hh] = s_new
            o_scr[rows, ln] = o
        return carry

    lax.fori_loop(0, tb, seq_body, 0, unroll=4)

    gates = _dot(h, w_in_ref[:, col_ga:col_ga + 2 * d_model])
    x1_ref[...] = _gated_merge_out(x, ya, gates[:, :d_model], gates[:, d_model:], o_scr[...],
                                   z_scr[:, 3 * hw:4 * hw], hgrn_norm_ref, w_hgrn_up_ref, w_out_ref,
                                   n_heads)


def _ffn_ple_kernel(x_ref, p_ref, g_ffn_ref, w_gate_ref, w_up_ref, w_down_ref, g_ple_ref,
                    w_ple_gate_ref, w_ple_proj_ref, g_final_ref, y_ref):
    x = x_ref[...]
    h2 = _bf(_rms(x, g_ffn_ref[...]))
    act = _silu(_dot(h2, w_gate_ref[...])) * _dot(h2, w_up_ref[...])
    x = x + _dot(_bf(act), w_down_ref[...])
    h3 = _bf(_rms(x, g_ple_ref[...]))
    gate = jax.nn.sigmoid(_dot(h3, w_ple_gate_ref[...]))
    emb = _dot(_bf(p_ref[...]), w_ple_proj_ref[...])
    x = x + gate * emb
    y_ref[...] = _rms(x, g_final_ref[...])


def _whole(_):
    return pl.BlockSpec(memory_space=pltpu.VMEM)


def _mixer_prompt(x, weights, to_cast, *, tile, chunk):
    bsz, seq, d_model = x.shape
    pw = weights[3].shape[1]
    hw = weights[5].shape[1]
    n_heads = hw // LANES
    ns = PROMPT_STREAMS
    assert seq % tile == 0 and tile % chunk == 0 and chunk % SUB == 0 and bsz % ns == 0
    nt = seq // tile
    n_steps = (bsz // ns) * nt
    slabs = []
    for w in to_cast:
        rep = 1 if w.shape[0] % (n_steps * 2 * SUBLANES) == 0 else 2
        rows = w.shape[0] * rep // n_steps
        assert rows * n_steps == w.shape[0] * rep and rows % (2 * SUBLANES) == 0
        slabs.append(pl.BlockSpec((rows, w.shape[1]), lambda b, t, rep=rep: ((b * nt + t) // rep, 0)))
    outs = pl.pallas_call(
        functools.partial(_mixer_prompt_kernel, chunk=chunk, n_cast=len(to_cast)),
        grid=(bsz // ns, nt),
        in_specs=([pl.BlockSpec((ns, tile, d_model), lambda b, t: (b, t, 0))]
                  + [_whole(w) for w in weights] + slabs),
        out_specs=[
            pl.BlockSpec((ns, tile, d_model), lambda b, t: (b, t, 0)),
            pl.BlockSpec((ns, POOL_HALO, pw), lambda b, t: (b, 0, 0)),
            pl.BlockSpec((ns, n_heads, LANES, LANES), lambda b, t: (b, 0, 0, 0)),
        ] + slabs,
        out_shape=[
            jax.ShapeDtypeStruct((bsz, seq, d_model), F32),
            jax.ShapeDtypeStruct((bsz, POOL_HALO, pw), F32),
            jax.ShapeDtypeStruct((bsz, n_heads, LANES, LANES), F32),
        ] + [jax.ShapeDtypeStruct(w.shape, BF16) for w in to_cast],
        scratch_shapes=[
            pltpu.VMEM((ns, POOL_HALO + tile, pw), F32),
            pltpu.VMEM((ns, tile, 4 * hw), F32),
            pltpu.VMEM((ns, tile, hw), F32),
            pltpu.VMEM((ns, n_heads, LANES, LANES), F32),
        ],
        compiler_params=pltpu.CompilerParams(
            dimension_semantics=("parallel", "arbitrary"), vmem_limit_bytes=VMEM_LIMIT),
        name="mixer_prompt",
    )(x, *weights, *to_cast)
    return outs[0], outs[1], outs[2], outs[3:]


def _mixer_sample(x, state_pool, state_hgrn, weights, *, tile_b):
    bsz, seq, d_model = x.shape
    _, buf, pw = state_pool.shape
    _, n_heads, dk, dv = state_hgrn.shape
    hw = n_heads * dv
    assert bsz % tile_b == 0 and seq == 8 and buf + 1 == POOL_HALO and dk == LANES and dv == LANES
    rows = tile_b * seq
    xf = x.reshape(bsz * seq, d_model)
    return pl.pallas_call(
        functools.partial(_mixer_sample_kernel, seq=seq),
        grid=(bsz // tile_b,),
        in_specs=[
            pl.BlockSpec((rows, d_model), lambda i: (i, 0)),
            pl.BlockSpec((tile_b, buf, pw), lambda i: (i, 0, 0)),
            pl.BlockSpec((tile_b, n_heads, dk, dv), lambda i: (i, 0, 0, 0)),
        ] + [_whole(w) for w in weights],
        out_specs=[
            pl.BlockSpec((rows, d_model), lambda i: (i, 0)),
            pl.BlockSpec((tile_b, POOL_HALO, pw), lambda i: (i, 0, 0)),
            pl.BlockSpec((tile_b, n_heads, dk, dv), lambda i: (i, 0, 0, 0)),
        ],
        out_shape=[
            jax.ShapeDtypeStruct((bsz * seq, d_model), F32),
            jax.ShapeDtypeStruct((bsz, POOL_HALO, pw), F32),
            jax.ShapeDtypeStruct((bsz, n_heads, dk, dv), F32),
        ],
        scratch_shapes=[
            pltpu.VMEM((tile_b, 1 + buf + seq, pw), F32),
            pltpu.VMEM((rows, 4 * hw), F32),
            pltpu.VMEM((rows, hw), F32),
            pltpu.VMEM((rows, hw), F32),
            pltpu.VMEM((rows, hw), F32),
            pltpu.VMEM((rows, hw), F32),
            pltpu.VMEM((rows, hw), F32),
        ],
        compiler_params=pltpu.CompilerParams(
            dimension_semantics=("parallel",), vmem_limit_bytes=VMEM_LIMIT),
        name="mixer_sample",
    )(xf, state_pool, state_hgrn, *weights)


def _ffn_ple(x, p, weights, *, tile):
    n, d_model = x.shape
    p_dim = p.shape[1]
    assert n % tile == 0
    return pl.pallas_call(
        _ffn_ple_kernel,
        grid=(n // tile,),
        in_specs=[
            pl.BlockSpec((tile, d_model), lambda i: (i, 0)),
            pl.BlockSpec((tile, p_dim), lambda i: (i, 0)),
        ] + [_whole(w) for w in weights],
        out_specs=pl.BlockSpec((tile, d_model), lambda i: (i, 0)),
        out_shape=jax.ShapeDtypeStruct((n, d_model), F32),
        compiler_params=pltpu.CompilerParams(
            dimension_semantics=("parallel",), vmem_limit_bytes=VMEM_LIMIT),
        name="ffn_ple",
    )(x, p, *weights)


def kernel(x_prompt, x_sample, state_pool, state_hgrn, p_prompt, p_sample, g_mix, w_in, w_pool_mix, pool_scale, hgrn_lb, hgrn_norm, w_pool_up, w_hgrn_up, w_out, g_ffn, w_ffn_gate, w_ffn_up, w_ffn_down, g_ple, w_ple_gate, w_ple_proj, g_final):
    depth = w_in.shape[0]
    assert depth == 1 and hgrn_lb.shape[0] == 2
    bsz, seq, d_model = x_prompt.shape
    dbsz, dseq, _ = x_sample.shape
    buf = state_pool.shape[2]

    mixer_w = (g_mix, _bf(w_in[0]), _bf(w_pool_mix[0]), pool_scale, hgrn_lb, hgrn_norm,
               _bf(w_pool_up[0]), _bf(w_hgrn_up[0]), _bf(w_out[0]))
    x1_p, pool_p, hgrn_p, (wg, wu, wd, wpg, wpp) = _mixer_prompt(
        x_prompt, mixer_w, (w_ffn_gate[0], w_ffn_up[0], w_ffn_down[0], w_ple_gate[0], w_ple_proj[0]),
        tile=512, chunk=128)
    ffn_w = (g_ffn, wg, wu, wd, g_ple, wpg, wpp, g_final.reshape(1, d_model))
    x1_s, pool_s, hgrn_s = _mixer_sample(x_sample, state_pool[0], state_hgrn[0], mixer_w, tile_b=16)

    y_p = _ffn_ple(x1_p.reshape(bsz * seq, d_model), p_prompt[0].reshape(bsz * seq, -1), ffn_w, tile=512)
    y_s = _ffn_ple(x1_s, p_sample[0].reshape(dbsz * dseq, -1), ffn_w, tile=512)

    return (y_p.reshape(bsz, seq, d_model), y_s.reshape(dbsz, dseq, d_model),
            pool_p[None, :, POOL_HALO - buf:, :], hgrn_p[None],
            pool_s[None, :, POOL_HALO - buf:, :], hgrn_s[None])
```

```python
import functools

import jax
import jax.numpy as jnp
from jax import lax
from jax.experimental import pallas as pl
from jax.experimental.pallas import tpu as pltpu

F32 = jnp.float32
BF16 = jnp.bfloat16

EPS = 1e-6
POOL_WINDOWS = (2, 4, 8, 16)
POOL_HALO = 16
SUB = 16
LANES = 128
SUBLANES = 8
VMEM_LIMIT = 56 * 1024 * 1024
GATE_PIECE = 256
PROMPT_STREAMS = 1
STREAM_LAG = 5

_NT = (((1,), (1,)), ((), ()))
_TN = (((0,), (0,)), ((), ()))


def _bf(x):
    return x.astype(BF16)


def _dot(a, b):
    return jnp.dot(a, b, preferred_element_type=F32)


def _rms(x, gain):
    ms = jnp.mean(x * x, axis=-1, keepdims=True)
    return x * lax.rsqrt(ms + EPS) * gain


def _silu(x):
    return x * jax.nn.sigmoid(x)


def _forget_lower_bound(lb_ref):
    a0 = lb_ref[0:1, :]
    a1 = lb_ref[1:2, :]
    m = jnp.maximum(a0, a1)
    e0 = jnp.exp(a0 - m)
    e1 = jnp.exp(a1 - m)
    return e0 / (e0 + e1)


def _scan_rows8(x, pos8):
    y = x.reshape(x.shape[0] // SUBLANES, SUBLANES, x.shape[1])
    s = 1
    while s < SUBLANES:
        y = y + jnp.where(pos8 >= s, pltpu.roll(y, s, axis=1), 0.0)
        s *= 2
    return y


def _last_row(y):
    return jnp.broadcast_to(y[..., SUBLANES - 1:SUBLANES, :], y.shape)


def _finish_head(a, v, s_prev, q_big, k_big, dec_row):
    vb = _bf(v)
    if a.shape[1] % LANES == 0:
        o = _dot(jnp.concatenate([_bf(a), _bf(q_big)], axis=1),
                 jnp.concatenate([vb, _bf(s_prev)], axis=0))
    else:
        o = _dot(_bf(a), vb) + _dot(_bf(q_big), _bf(s_prev))
    ds = lax.dot_general(_bf(k_big), vb, _TN, preferred_element_type=F32)
    dk = s_prev.shape[0]
    dec = jnp.transpose(jnp.broadcast_to(dec_row, (dk, dk)))
    return o, dec * s_prev + ds


def _scale_blocks(x, factors):
    out = []
    for i, f in enumerate(factors):
        blk = x[i * SUB:(i + 1) * SUB]
        out.append(blk if f is None else blk * jnp.concatenate([f] * (SUB // SUBLANES), axis=0))
    return jnp.concatenate(out, axis=0)


def _chunk_head(q, k, logf, v, s_prev, pos8, masks):
    n = q.shape[0]
    nb = n // SUB
    y = _scan_rows8(logf, pos8)
    cs_blk, tots = [], []
    for i in range(nb):
        lo = y[2 * i]
        hi = y[2 * i + 1] + _last_row(y[2 * i])
        cs_blk += [lo, hi]
        tots.append(_last_row(hi))
    cs = jnp.concatenate(cs_blk, axis=0)
    rs = jnp.concatenate([t for t in tots for _ in range(SUB // SUBLANES)], axis=0) - cs
    ecs = jnp.exp(cs)
    qe = q * ecs
    ke = k * jnp.exp(-cs)
    kd = k * jnp.exp(rs)

    def span(lo, hi):
        acc = None
        for m in range(lo, hi):
            acc = tots[m] if acc is None else acc + tots[m]
        return acc

    def expo(t):
        return None if t is None else jnp.exp(t)

    a01 = lax.dot_general(_bf(qe), _bf(jnp.concatenate([ke, kd], axis=0)), _NT,
                          preferred_element_type=F32)
    a = jnp.where(masks[0], a01[:, :n], jnp.where(masks[1], a01[:, n:], 0.0))
    lvl = 2
    w = 2
    while w * SUB < n:
        qf = [expo(span((i // w) * w, i)) if (i // w) % 2 == 1 else None for i in range(nb)]
        kf = [expo(span(i + 1, (i // w + 1) * w)) if (i // w) % 2 == 0 else None for i in range(nb)]
        a_l = lax.dot_general(_bf(_scale_blocks(qe, qf)), _bf(_scale_blocks(kd, kf)), _NT,
                              preferred_element_type=F32)
        a = jnp.where(masks[lvl], a_l, a)
        lvl += 1
        w *= 2
    q_big = _scale_blocks(qe, [expo(span(0, i)) for i in range(nb)])
    k_big = _scale_blocks(kd, [expo(span(i + 1, nb)) for i in range(nb)])
    dec_row = jnp.exp(span(0, nb))[0:1, :]
    return _finish_head(a, v, s_prev, q_big, k_big, dec_row)


def _gated_merge_out(x, ya, ga, gb, o_raw, zg, hgrn_norm_ref, w_hgrn_up_ref, w_out_ref, n_heads):
    pieces = []
    for hh in range(n_heads):
        ln = slice(hh * LANES, (hh + 1) * LANES)
        oh = o_raw[:, ln]
        oh = oh * lax.rsqrt(jnp.mean(oh * oh, axis=-1, keepdims=True) + EPS)
        pieces.append(oh * hgrn_norm_ref[:, ln])
    o = jnp.concatenate(pieces, axis=-1) * _silu(zg)
    yb = _dot(_bf(o), w_hgrn_up_ref[...])
    merged = jax.nn.sigmoid(ga) * ya + jax.nn.sigmoid(gb) * yb
    return x + _dot(_bf(merged), w_out_ref[...])


def _pool_project(pooled, w_pool_mix_ref, pool_scale_ref, w_pool_up_ref):
    pieces = [_dot(_bf(pg), w_pool_mix_ref[g]) for g, pg in enumerate(pooled)]
    pool_out = jnp.concatenate(pieces, axis=-1) * pool_scale_ref[...]
    return _dot(_bf(pool_out), w_pool_up_ref[...])


def _pool_branch(u, shifted, inv_cnt, w_pool_mix_ref, pool_scale_ref, w_pool_up_ref):
    pooled = []
    for g, w in enumerate(POOL_WINDOWS):
        ln = slice(g * LANES, (g + 1) * LANES)
        ug = u[:, ln]
        acc = ug
        for j in range(1, w):
            acc = acc + shifted(j, ln)
        pooled.append(acc * inv_cnt(w) - ug)
    return _pool_project(pooled, w_pool_mix_ref, pool_scale_ref, w_pool_up_ref)


def _prompt_stream(s, t, x_ref, g_mix_ref, w_in_ref, w_pool_mix_ref, pool_scale_ref, lb_ref,
                   hgrn_norm_ref, w_pool_up_ref, w_hgrn_up_ref, w_out_ref, x1_ref,
                   u_scr, z_scr, o_scr, s_scr, chunk, pos8, masks):
    _, tm, d_model = x_ref.shape
    pw = u_scr.shape[2]
    hw = o_scr.shape[2]
    n_heads = hw // LANES
    col_ga = pw + 4 * hw
    st = {"gates": []}

    def norm():
        st["h"] = _bf(_rms(x_ref[s], g_mix_ref[...]))
        st["lb"] = _forget_lower_bound(lb_ref)

    def proj_u():
        u_scr[s, POOL_HALO:POOL_HALO + tm, :] = _dot(st["h"], w_in_ref[:, 0:pw])

    def proj_z(j):
        def run():
            z_scr[s, :, j * hw:(j + 1) * hw] = _dot(st["h"], w_in_ref[:, pw + j * hw:pw + (j + 1) * hw])
        return run

    def pool():
        seen = lax.broadcasted_iota(jnp.int32, (tm, 1), 0) + (t * tm + 1)

        def shifted(j, ln):
            return u_scr[s, pl.ds(POOL_HALO - j, tm), ln]

        def inv_cnt(w):
            return 1.0 / jnp.minimum(seen, w).astype(F32)

        st["ya"] = _pool_branch(u_scr[s, POOL_HALO:POOL_HALO + tm, :], shifted, inv_cnt,
                                w_pool_mix_ref, pool_scale_ref, w_pool_up_ref)
        u_scr[s, 0:POOL_HALO, :] = u_scr[s, tm:tm + POOL_HALO, :]

    gate_cols = [(col_ga + j * GATE_PIECE) for j in range(2 * d_model // GATE_PIECE)]
    steps = [(c, hh) for c in range(tm // chunk) for hh in range(n_heads)]

    def step(k):
        c, hh = steps[k]

        def run():
            rows = slice(c * chunk, (c + 1) * chunk)
            ln = slice(hh * LANES, (hh + 1) * LANES)
            zq = z_scr[s, rows, hh * LANES:(hh + 1) * LANES]
            zf = z_scr[s, rows, hw + hh * LANES:hw + (hh + 1) * LANES]
            v = z_scr[s, rows, 2 * hw + hh * LANES:2 * hw + (hh + 1) * LANES]
            lbh = st["lb"][:, ln]
            fg = lbh + (1.0 - lbh) * jax.nn.sigmoid(zf)
            o, s_new = _chunk_head(_silu(zq), 1.0 - fg, jnp.log(fg), v, s_scr[s, hh], pos8, masks)
            s_scr[s, hh] = s_new
            o_scr[s, rows, ln] = o
            while len(st["gates"]) < (k + 1) * len(gate_cols) // len(steps):
                c0 = gate_cols[len(st["gates"])]
                st["gates"].append(_dot(st["h"], w_in_ref[:, c0:c0 + GATE_PIECE]))
        return run

    def tail():
        gates = jnp.concatenate(st["gates"], axis=1)
        x1_ref[s] = _gated_merge_out(x_ref[s], st["ya"], gates[:, :d_model], gates[:, d_model:],
                                     o_scr[s], z_scr[s, :, 3 * hw:4 * hw], hgrn_norm_ref,
                                     w_hgrn_up_ref, w_out_ref, n_heads)

    return ([norm, proj_u] + [proj_z(j) for j in range(4)] + [pool]
            + [step(k) for k in range(len(steps))] + [tail])


def _mixer_prompt_kernel(x_ref, g_mix_ref, w_in_ref, w_pool_mix_ref, pool_scale_ref, lb_ref,
                         hgrn_norm_ref, w_pool_up_ref, w_hgrn_up_ref, w_out_ref, *rest, chunk, n_cast):
    cast_in = rest[:n_cast]
    x1_ref, pool_out_ref, hgrn_out_ref = rest[n_cast:n_cast + 3]
    cast_out = rest[n_cast + 3:2 * n_cast + 3]
    u_scr, z_scr, o_scr, s_scr = rest[2 * n_cast + 3:]
    n_seq = x_ref.shape[0]
    t = pl.program_id(1)

    for src, dst in zip(cast_in, cast_out):
        dst[...] = _bf(src[...])

    @pl.when(t == 0)
    def _():
        u_scr[:, 0:POOL_HALO, :] = jnp.zeros((n_seq, POOL_HALO, u_scr.shape[2]), F32)
        s_scr[...] = jnp.zeros(s_scr.shape, F32)

    pos8 = lax.broadcasted_iota(jnp.int32, (1, SUBLANES, LANES), 1)
    ti = lax.broadcasted_iota(jnp.int32, (chunk, chunk), 0)
    si = lax.broadcasted_iota(jnp.int32, (chunk, chunk), 1)
    masks = [((ti // SUB) == (si // SUB)) & (si <= ti)]
    b = SUB
    while b < chunk:
        masks.append(((ti // (2 * b)) == (si // (2 * b))) & ((ti & b) != 0) & ((si & b) == 0))
        b *= 2

    streams = [_prompt_stream(s, t, x_ref, g_mix_ref, w_in_ref, w_pool_mix_ref, pool_scale_ref, lb_ref,
                              hgrn_norm_ref, w_pool_up_ref, w_hgrn_up_ref, w_out_ref, x1_ref,
                              u_scr, z_scr, o_scr, s_scr, chunk, pos8, masks) for s in range(n_seq)]
    for phases in streams:
        phases[0]()
    n = len(streams[0])
    for i in range(1, n + STREAM_LAG * (n_seq - 1)):
        for s, phases in enumerate(streams):
            if 1 <= i - s * STREAM_LAG < n:
                phases[i - s * STREAM_LAG]()

    @pl.when(t == pl.num_programs(1) - 1)
    def _():
        pool_out_ref[...] = u_scr[:, 0:POOL_HALO, :]
        hgrn_out_ref[...] = s_scr[...]


def _mixer_sample_kernel(x_ref, sp_ref, sh_ref, g_mix_ref, w_in_ref, w_pool_mix_ref, pool_scale_ref,
                         lb_ref, hgrn_norm_ref, w_pool_up_ref, w_hgrn_up_ref, w_out_ref,
                         x1_ref, pool_out_ref, hgrn_out_ref,
                         e_scr, z_scr, o_scr, qe_scr, ke_scr, kd_scr, dec_scr, *, seq):
    rows_n, d_model = x_ref.shape
    buf, tb, pw = sp_ref.shape
    hw = o_scr.shape[1]
    n_heads = hw // LANES

    x = x_ref[...]
    h = _bf(_rms(x, g_mix_ref[...]))

    u = _dot(h, w_in_ref[:, 0:pw])
    pooled = []
    for g, w in enumerate(POOL_WINDOWS):
        ln = slice(g * LANES, (g + 1) * LANES)
        e_scr[g] = u[:, ln]
        ext = [sp_ref[e, :, ln] for e in range(buf)]
        ext += [e_scr[g, pl.ds(tt, tb, stride=seq), :] for tt in range(seq)]
        for e in range(buf):
            pool_out_ref[e, :, ln] = ext[seq + e]
        for tt in range(seq):
            acc = ext[buf + tt]
            for j in range(1, w):
                acc = acc + ext[buf + tt - j]
            e_scr[g, pl.ds(tt, tb, stride=seq), :] = acc * (1.0 / w) - ext[buf + tt]
        pooled.append(e_scr[g])
    col_ga = pw + 4 * hw
    ya = _pool_project(pooled, w_pool_mix_ref, pool_scale_ref, w_pool_up_ref)

    z_scr[...] = _dot(h, w_in_ref[:, pw:pw + 4 * hw])
    lb = _forget_lower_bound(lb_ref)
    pos8 = lax.broadcasted_iota(jnp.int32, (1, SUBLANES, LANES), 1)
    for hh in range(n_heads):
        ln = slice(hh * LANES, (hh + 1) * LANES)
        zq = z_scr[:, hh * LANES:(hh + 1) * LANES]
        zf = z_scr[:, hw + hh * LANES:hw + (hh + 1) * LANES]
        lbh = lb[:, ln]
        fg = lbh + (1.0 - lbh) * jax.nn.sigmoid(zf)
        k = 1.0 - fg
        q = _silu(zq)
        y = _scan_rows8(jnp.log(fg), pos8)
        tot = _last_row(y)
        cs = y.reshape(rows_n, LANES)
        qe_scr[:, ln] = q * jnp.exp(cs)
        ke_scr[:, ln] = k * jnp.exp(-cs)
        kd_scr[:, ln] = k * jnp.exp((tot - y).reshape(rows_n, LANES))
        dec_scr[:, ln] = jnp.exp(tot).reshape(rows_n, LANES)

    ti = lax.broadcasted_iota(jnp.int32, (seq, seq), 0)
    si = lax.broadcasted_iota(jnp.int32, (seq, seq), 1)
    causal = si <= ti

    def seq_body(b, carry):
        r0 = pl.multiple_of(b * seq, seq)
        rows = pl.ds(r0, seq)
        for hh in range(n_heads):
            ln = slice(hh * LANES, (hh + 1) * LANES)
            qe = qe_scr[rows, ln]
            v = z_scr[rows, 2 * hw + hh * LANES:2 * hw + (hh + 1) * LANES]
            a = lax.dot_general(_bf(qe), _bf(ke_scr[rows, ln]), _NT, preferred_element_type=F32)
            o, s_new = _finish_head(jnp.where(causal, a, 0.0), v, sh_ref[b, hh], qe,
                                    kd_scr[rows, ln], dec_scr[pl.ds(r0, 1), ln])
            hgrn_out_ref[b, hh] = s_new
            o_scr[rows, ln] = o
        return carry

    lax.fori_loop(0, tb, seq_body, 0, unroll=4)

    gates = _dot(h, w_in_ref[:, col_ga:col_ga + 2 * d_model])
    x1_ref[...] = _gated_merge_out(x, ya, gates[:, :d_model], gates[:, d_model:], o_scr[...],
                                   z_scr[:, 3 * hw:4 * hw], hgrn_norm_ref, w_hgrn_up_ref, w_out_ref,
                                   n_heads)


def _ffn_ple_kernel(xp_ref, xs_ref, pp_ref, ps_ref, g_ffn_ref, w_gate_ref, w_up_ref, w_down_ref,
                    g_ple_ref, w_ple_gate_ref, w_ple_proj_ref, g_final_ref, yp_ref, ys_ref, *, n_prompt):
    def tile(x_ref, p_ref, y_ref):
        x = x_ref[...]
        h2 = _bf(_rms(x, g_ffn_ref[...]))
        act = _silu(_dot(h2, w_gate_ref[...])) * _dot(h2, w_up_ref[...])
        x = x + _dot(_bf(act), w_down_ref[...])
        h3 = _bf(_rms(x, g_ple_ref[...]))
        gate = jax.nn.sigmoid(_dot(h3, w_ple_gate_ref[...]))
        emb = _dot(_bf(p_ref[...]), w_ple_proj_ref[...])
        x = x + gate * emb
        y_ref[...] = _rms(x, g_final_ref[...])

    i = pl.program_id(0)

    @pl.when(i < n_prompt)
    def _():
        tile(xp_ref, pp_ref, yp_ref)

    @pl.when(i >= n_prompt)
    def _():
        tile(xs_ref, ps_ref, ys_ref)


def _whole(_):
    return pl.BlockSpec(memory_space=pltpu.VMEM)


def _mixer_prompt(x, weights, to_cast, *, tile, chunk):
    bsz, seq, d_model = x.shape
    pw = weights[3].shape[1]
    hw = weights[5].shape[1]
    n_heads = hw // LANES
    ns = PROMPT_STREAMS
    assert seq % tile == 0 and tile % chunk == 0 and chunk % SUB == 0 and bsz % ns == 0
    nt = seq // tile
    n_steps = (bsz // ns) * nt
    slabs = []
    for w in to_cast:
        rep = 1 if w.shape[0] % (n_steps * 2 * SUBLANES) == 0 else 2
        rows = w.shape[0] * rep // n_steps
        assert rows * n_steps == w.shape[0] * rep and rows % (2 * SUBLANES) == 0
        slabs.append(pl.BlockSpec((rows, w.shape[1]), lambda b, t, rep=rep: ((b * nt + t) // rep, 0)))
    outs = pl.pallas_call(
        functools.partial(_mixer_prompt_kernel, chunk=chunk, n_cast=len(to_cast)),
        grid=(bsz // ns, nt),
        in_specs=([pl.BlockSpec((ns, tile, d_model), lambda b, t: (b, t, 0))]
                  + [_whole(w) for w in weights] + slabs),
        out_specs=[
            pl.BlockSpec((ns, tile, d_model), lambda b, t: (b, t, 0)),
            pl.BlockSpec((ns, POOL_HALO, pw), lambda b, t: (b, 0, 0)),
            pl.BlockSpec((ns, n_heads, LANES, LANES), lambda b, t: (b, 0, 0, 0)),
        ] + slabs,
        out_shape=[
            jax.ShapeDtypeStruct((bsz, seq, d_model), F32),
            jax.ShapeDtypeStruct((bsz, POOL_HALO, pw), F32),
            jax.ShapeDtypeStruct((bsz, n_heads, LANES, LANES), F32),
        ] + [jax.ShapeDtypeStruct(w.shape, BF16) for w in to_cast],
        scratch_shapes=[
            pltpu.VMEM((ns, POOL_HALO + tile, pw), F32),
            pltpu.VMEM((ns, tile, 4 * hw), F32),
            pltpu.VMEM((ns, tile, hw), F32),
            pltpu.VMEM((ns, n_heads, LANES, LANES), F32),
        ],
        compiler_params=pltpu.CompilerParams(
            dimension_semantics=("parallel", "arbitrary"), vmem_limit_bytes=VMEM_LIMIT),
        name="mixer_prompt",
    )(x, *weights, *to_cast)
    return outs[0], outs[1], outs[2], outs[3:]


def _mixer_sample(x, state_pool, state_hgrn, weights, *, tile_b):
    bsz, seq, d_model = x.shape
    buf, _, pw = state_pool.shape
    _, n_heads, dk, dv = state_hgrn.shape
    hw = n_heads * dv
    assert bsz % tile_b == 0 and seq == SUBLANES and buf >= max(POOL_WINDOWS) - 1
    assert dk == LANES and dv == LANES
    rows = tile_b * seq
    xf = x.reshape(bsz * seq, d_model)
    return pl.pallas_call(
        functools.partial(_mixer_sample_kernel, seq=seq),
        grid=(bsz // tile_b,),
        in_specs=[
            pl.BlockSpec((rows, d_model), lambda i: (i, 0)),
            pl.BlockSpec((buf, tile_b, pw), lambda i: (0, i, 0)),
            pl.BlockSpec((tile_b, n_heads, dk, dv), lambda i: (i, 0, 0, 0)),
        ] + [_whole(w) for w in weights],
        out_specs=[
            pl.BlockSpec((rows, d_model), lambda i: (i, 0)),
            pl.BlockSpec((buf, tile_b, pw), lambda i: (0, i, 0)),
            pl.BlockSpec((tile_b, n_heads, dk, dv), lambda i: (i, 0, 0, 0)),
        ],
        out_shape=[
            jax.ShapeDtypeStruct((bsz * seq, d_model), F32),
            jax.ShapeDtypeStruct((buf, bsz, pw), F32),
            jax.ShapeDtypeStruct((bsz, n_heads, dk, dv), F32),
        ],
        scratch_shapes=[
            pltpu.VMEM((pw // LANES, rows, LANES), F32),
            pltpu.VMEM((rows, 4 * hw), F32),
            pltpu.VMEM((rows, hw), F32),
            pltpu.VMEM((rows, hw), F32),
            pltpu.VMEM((rows, hw), F32),
            pltpu.VMEM((rows, hw), F32),
            pltpu.VMEM((rows, hw), F32),
        ],
        compiler_params=pltpu.CompilerParams(
            dimension_semantics=("parallel",), vmem_limit_bytes=VMEM_LIMIT),
        name="mixer_sample",
    )(xf, state_pool, state_hgrn, *weights)


def _ffn_ple(x_p, x_s, p_p, p_s, weights, *, tile):
    (n_p, d_model), n_s = x_p.shape, x_s.shape[0]
    p_dim = p_p.shape[1]
    assert n_p % tile == 0 and n_s % tile == 0
    tp, ts = n_p // tile, n_s // tile

    def prompt_idx(i):
        return (jnp.minimum(i, tp - 1), 0)

    def sample_idx(i):
        return (jnp.maximum(i - tp, 0), 0)

    return pl.pallas_call(
        functools.partial(_ffn_ple_kernel, n_prompt=tp),
        grid=(tp + ts,),
        in_specs=[
            pl.BlockSpec((tile, d_model), prompt_idx),
            pl.BlockSpec((tile, d_model), sample_idx),
            pl.BlockSpec((tile, p_dim), prompt_idx),
            pl.BlockSpec((tile, p_dim), sample_idx),
        ] + [_whole(w) for w in weights],
        out_specs=[
            pl.BlockSpec((tile, d_model), prompt_idx),
            pl.BlockSpec((tile, d_model), sample_idx),
        ],
        out_shape=[
            jax.ShapeDtypeStruct((n_p, d_model), F32),
            jax.ShapeDtypeStruct((n_s, d_model), F32),
        ],
        compiler_params=pltpu.CompilerParams(
            dimension_semantics=("arbitrary",), vmem_limit_bytes=VMEM_LIMIT),
        name="ffn_ple",
    )(x_p, x_s, p_p, p_s, *weights)


def kernel(x_prompt, x_sample, state_pool, state_hgrn, p_prompt, p_sample, g_mix, w_in, w_pool_mix, pool_scale, hgrn_lb, hgrn_norm, w_pool_up, w_hgrn_up, w_out, g_ffn, w_ffn_gate, w_ffn_up, w_ffn_down, g_ple, w_ple_gate, w_ple_proj, g_final):
    depth = w_in.shape[0]
    assert depth == 1 and hgrn_lb.shape[0] == 2
    bsz, seq, d_model = x_prompt.shape
    dbsz, dseq, _ = x_sample.shape
    buf = state_pool.shape[2]

    mixer_w = (g_mix, _bf(w_in[0]), _bf(w_pool_mix[0]), pool_scale, hgrn_lb, hgrn_norm,
               _bf(w_pool_up[0]), _bf(w_hgrn_up[0]), _bf(w_out[0]))
    x1_p, pool_p, hgrn_p, (wg, wu, wd, wpg, wpp) = _mixer_prompt(
        x_prompt, mixer_w, (w_ffn_gate[0], w_ffn_up[0], w_ffn_down[0], w_ple_gate[0], w_ple_proj[0]),
        tile=512, chunk=128)
    ffn_w = (g_ffn, wg, wu, wd, g_ple, wpg, wpp, g_final.reshape(1, d_model))
    x1_s, pool_s, hgrn_s = _mixer_sample(x_sample, jnp.swapaxes(state_pool[0], 0, 1), state_hgrn[0],
                                         mixer_w, tile_b=16)

    y_p, y_s = _ffn_ple(x1_p.reshape(bsz * seq, d_model), x1_s,
                        p_prompt[0].reshape(bsz * seq, -1), p_sample[0].reshape(dbsz * dseq, -1),
                        ffn_w, tile=512)

    return (y_p.reshape(bsz, seq, d_model), y_s.reshape(dbsz, dseq, d_model),
            pool_p[None, :, POOL_HALO - buf:, :], hgrn_p[None],
            jnp.swapaxes(pool_s, 0, 1)[None], hgrn_s[None])
```

```python
import functools

import jax
import jax.numpy as jnp
from jax import lax
from jax.experimental import pallas as pl
from jax.experimental.pallas import tpu as pltpu

F32 = jnp.float32
BF16 = jnp.bfloat16

EPS = 1e-6
POOL_WINDOWS = (2, 4, 8, 16)
POOL_HALO = 16
SUB = 16
LANES = 128
SUBLANES = 8
VMEM_LIMIT = 62 * 1024 * 1024
GATE_PIECE = 256
POOL_ROWS = 64
PROMPT_STREAMS = 2

_NT = (((1,), (1,)), ((), ()))
_TN = (((0,), (0,)), ((), ()))


def _bf(x):
    return x.astype(BF16)


def _dot(a, b):
    return jnp.dot(a, b, preferred_element_type=F32)


def _rms(x, gain):
    ms = jnp.mean(x * x, axis=-1, keepdims=True)
    return x * lax.rsqrt(ms + EPS) * gain


def _silu(x):
    return x * jax.nn.sigmoid(x)


def _forget_lower_bound(lb_ref):
    a0 = lb_ref[0:1, :]
    a1 = lb_ref[1:2, :]
    m = jnp.maximum(a0, a1)
    e0 = jnp.exp(a0 - m)
    e1 = jnp.exp(a1 - m)
    return e0 / (e0 + e1)


def _scan_rows8(x, pos8):
    y = x.reshape(x.shape[0] // SUBLANES, SUBLANES, x.shape[1])
    s = 1
    while s < SUBLANES:
        y = y + jnp.where(pos8 >= s, pltpu.roll(y, s, axis=1), 0.0)
        s *= 2
    return y


def _last_row(y):
    return jnp.broadcast_to(y[..., SUBLANES - 1:SUBLANES, :], y.shape)


def _finish_head(a, v, s_prev, q_big, k_big, dec_row):
    vb = _bf(v)
    if a.shape[1] % LANES == 0:
        o = _dot(jnp.concatenate([_bf(a), _bf(q_big)], axis=1),
                 jnp.concatenate([vb, _bf(s_prev)], axis=0))
    else:
        o = _dot(_bf(a), vb) + _dot(_bf(q_big), _bf(s_prev))
    ds = lax.dot_general(_bf(k_big), vb, _TN, preferred_element_type=F32)
    dk = s_prev.shape[0]
    dec = jnp.transpose(jnp.broadcast_to(dec_row, (dk, dk)))
    return o, dec * s_prev + ds


def _scale_blocks(x, factors):
    out = []
    for i, f in enumerate(factors):
        blk = x[i * SUB:(i + 1) * SUB]
        out.append(blk if f is None else blk * jnp.concatenate([f] * (SUB // SUBLANES), axis=0))
    return jnp.concatenate(out, axis=0)


def _chunk_front(q, k, logf, pos8):
    n = q.shape[0]
    nb = n // SUB
    y = _scan_rows8(logf, pos8)
    cs_blk, tots = [], []
    for i in range(nb):
        lo = y[2 * i]
        hi = y[2 * i + 1] + _last_row(y[2 * i])
        cs_blk += [lo, hi]
        tots.append(_last_row(hi))
    cs = jnp.concatenate(cs_blk, axis=0)
    rs = jnp.concatenate([t for t in tots for _ in range(SUB // SUBLANES)], axis=0) - cs
    ecs = jnp.exp(cs)
    qe = q * ecs
    ke = k * jnp.exp(-cs)
    kd = k * jnp.exp(rs)

    def span(lo, hi):
        acc = None
        for m in range(lo, hi):
            acc = tots[m] if acc is None else acc + tots[m]
        return acc

    def expo(t):
        return None if t is None else jnp.exp(t)

    levels = []
    w = 2
    while w * SUB < n:
        qf = [expo(span((i // w) * w, i)) if (i // w) % 2 == 1 else None for i in range(nb)]
        kf = [expo(span(i + 1, (i // w + 1) * w)) if (i // w) % 2 == 0 else None for i in range(nb)]
        levels.append((_bf(_scale_blocks(qe, qf)), _bf(_scale_blocks(kd, kf))))
        w *= 2
    return {
        "q01": _bf(qe), "k01": _bf(jnp.concatenate([ke, kd], axis=0)),
        "levels": levels,
        "q_big": _bf(_scale_blocks(qe, [expo(span(0, i)) for i in range(nb)])),
        "k_big": _bf(_scale_blocks(kd, [expo(span(i + 1, nb)) for i in range(nb)])),
        "dec_row": jnp.exp(span(0, nb))[0:1, :],
    }


def _chunk_back(fr, v, s_prev, masks):
    n = fr["q01"].shape[0]
    a01 = lax.dot_general(fr["q01"], fr["k01"], _NT, preferred_element_type=F32)
    a = jnp.where(masks[0], a01[:, :n], jnp.where(masks[1], a01[:, n:], 0.0))
    for lvl, (q_l, k_l) in enumerate(fr["levels"]):
        a_l = lax.dot_general(q_l, k_l, _NT, preferred_element_type=F32)
        a = jnp.where(masks[lvl + 2], a_l, a)
    return _finish_head(a, v, s_prev, fr["q_big"], fr["k_big"], fr["dec_row"])


def _gated_merge_out(x, ya, ga, gb, o_raw, zg, hgrn_norm_ref, w_hgrn_up_ref, w_out_ref, n_heads):
    pieces = []
    for hh in range(n_heads):
        ln = slice(hh * LANES, (hh + 1) * LANES)
        oh = o_raw[:, ln]
        oh = oh * lax.rsqrt(jnp.mean(oh * oh, axis=-1, keepdims=True) + EPS)
        pieces.append(oh * hgrn_norm_ref[:, ln])
    o = jnp.concatenate(pieces, axis=-1) * _silu(zg)
    yb = _dot(_bf(o), w_hgrn_up_ref[...])
    merged = jax.nn.sigmoid(ga) * ya + jax.nn.sigmoid(gb) * yb
    return x + _dot(_bf(merged), w_out_ref[...])


def _pool_project(pooled, w_pool_mix_ref, pool_scale_ref, w_pool_up_ref):
    pieces = [_dot(_bf(pg), w_pool_mix_ref[g]) for g, pg in enumerate(pooled)]
    pool_out = jnp.concatenate(pieces, axis=-1) * pool_scale_ref[...]
    return _dot(_bf(pool_out), w_pool_up_ref[...])


def _pool_branch(u, shifted, inv_cnt, w_pool_mix_ref, pool_scale_ref, w_pool_up_ref):
    pooled = []
    for g, w in enumerate(POOL_WINDOWS):
        ln = slice(g * LANES, (g + 1) * LANES)
        ug = u[:, ln]
        acc = ug
        for j in range(1, w):
            acc = acc + shifted(j, ln)
        pooled.append(acc * inv_cnt(w) - ug)
    return _pool_project(pooled, w_pool_mix_ref, pool_scale_ref, w_pool_up_ref)


def _prompt_stream(s, t, x_ref, g_mix_ref, w_in_ref, w_pool_mix_ref, pool_scale_ref, lb_ref,
                   hgrn_norm_ref, w_pool_up_ref, w_hgrn_up_ref, w_out_ref, x1_ref,
                   u_scr, z_scr, o_scr, s_scr, chunk, pos8, masks):
    _, tm, d_model = x_ref.shape
    pw = u_scr.shape[2]
    hw = o_scr.shape[2]
    n_heads = hw // LANES
    col_ga = pw + 4 * hw
    st = {"gates": [], "pooled": [None] * len(POOL_WINDOWS)}

    def norm():
        st["h"] = _bf(_rms(x_ref[s], g_mix_ref[...]))
        st["lb"] = _forget_lower_bound(lb_ref)

    def proj_u():
        u_scr[s, POOL_HALO:POOL_HALO + tm, :] = _dot(st["h"], w_in_ref[:, 0:pw])

    def proj_z(j):
        def run():
            z_scr[s, :, j * hw:(j + 1) * hw] = _dot(st["h"], w_in_ref[:, pw + j * hw:pw + (j + 1) * hw])
        return run

    def pool_group(g):
        def run():
            w = POOL_WINDOWS[g]
            ln = slice(g * LANES, (g + 1) * LANES)
            out = []
            for r0 in range(0, tm, POOL_ROWS):
                seen = lax.broadcasted_iota(jnp.int32, (POOL_ROWS, 1), 0) + (t * tm + r0 + 1)
                ug = u_scr[s, POOL_HALO + r0:POOL_HALO + r0 + POOL_ROWS, ln]
                acc = ug
                for j in range(1, w):
                    acc = acc + u_scr[s, pl.ds(POOL_HALO + r0 - j, POOL_ROWS), ln]
                out.append(acc * (1.0 / jnp.minimum(seen, w).astype(F32)) - ug)
            st["pooled"][g] = jnp.concatenate(out, axis=0)
        return run

    def pool():
        st["ya"] = _pool_project(st["pooled"], w_pool_mix_ref, pool_scale_ref, w_pool_up_ref)
        u_scr[s, 0:POOL_HALO, :] = u_scr[s, tm:tm + POOL_HALO, :]

    steps = [(c, hh) for c in range(tm // chunk) for hh in range(n_heads)]

    def front(k):
        c, hh = steps[k]

        def run():
            rows = slice(c * chunk, (c + 1) * chunk)
            zq = z_scr[s, rows, hh * LANES:(hh + 1) * LANES]
            zf = z_scr[s, rows, hw + hh * LANES:hw + (hh + 1) * LANES]
            lbh = st["lb"][:, hh * LANES:(hh + 1) * LANES]
            fg = lbh + (1.0 - lbh) * jax.nn.sigmoid(zf)
            st["front"][k] = _chunk_front(_silu(zq), 1.0 - fg, jnp.log(fg), pos8)
        return run

    def back(k):
        c, hh = steps[k]

        def run():
            rows = slice(c * chunk, (c + 1) * chunk)
            v = z_scr[s, rows, 2 * hw + hh * LANES:2 * hw + (hh + 1) * LANES]
            o, s_new = _chunk_back(st["front"][k], v, s_scr[s, hh], masks)
            s_scr[s, hh] = s_new
            o_scr[s, rows, hh * LANES:(hh + 1) * LANES] = o
        return run

    def gate(c0):
        def run():
            st["gates"].append(_dot(st["h"], w_in_ref[:, c0:c0 + GATE_PIECE]))
        return run

    def tail():
        gates = jnp.concatenate(st["gates"], axis=1)
        x1_ref[s] = _gated_merge_out(x_ref[s], st["ya"], gates[:, :d_model], gates[:, d_model:],
                                     o_scr[s], z_scr[s, :, 3 * hw:4 * hw], hgrn_norm_ref,
                                     w_hgrn_up_ref, w_out_ref, n_heads)

    st["front"] = [None] * len(steps)
    gate_cols = [(col_ga + j * GATE_PIECE) for j in range(2 * d_model // GATE_PIECE)]
    return [
        [[norm]],
        [[proj_u] + [proj_z(j) for j in range(4)]],
        [[front(k) for k in range(len(steps))], [pool_group(g) for g in reversed(range(len(POOL_WINDOWS)))]],
        [[back(k) for k in range(len(steps))], [gate(c0) for c0 in gate_cols] + [pool]],
        [[tail]],
    ]


def _merge_evenly(lists):
    keyed = [((i + 0.5) / len(lst), n, fn) for n, lst in enumerate(lists) for i, fn in enumerate(lst)]
    return [fn for _, _, fn in sorted(keyed, key=lambda e: e[:2])]


def _mixer_prompt_kernel(x_ref, g_mix_ref, w_in_ref, w_pool_mix_ref, pool_scale_ref, lb_ref,
                         hgrn_norm_ref, w_pool_up_ref, w_hgrn_up_ref, w_out_ref, *rest, chunk, n_cast):
    cast_in = rest[:n_cast]
    x1_ref, pool_out_ref, hgrn_out_ref = rest[n_cast:n_cast + 3]
    cast_out = rest[n_cast + 3:2 * n_cast + 3]
    u_scr, z_scr, o_scr, s_scr = rest[2 * n_cast + 3:]
    n_seq = x_ref.shape[0]
    t = pl.program_id(1)

    for src, dst in zip(cast_in, cast_out):
        dst[...] = _bf(src[...])

    @pl.when(t == 0)
    def _():
        u_scr[:, 0:POOL_HALO, :] = jnp.zeros((n_seq, POOL_HALO, u_scr.shape[2]), F32)
        s_scr[...] = jnp.zeros(s_scr.shape, F32)

    pos8 = lax.broadcasted_iota(jnp.int32, (1, SUBLANES, LANES), 1)
    ti = lax.broadcasted_iota(jnp.int32, (chunk, chunk), 0)
    si = lax.broadcasted_iota(jnp.int32, (chunk, chunk), 1)
    masks = [((ti // SUB) == (si // SUB)) & (si <= ti)]
    b = SUB
    while b < chunk:
        masks.append(((ti // (2 * b)) == (si // (2 * b))) & ((ti & b) != 0) & ((si & b) == 0))
        b *= 2

    streams = [_prompt_stream(s, t, x_ref, g_mix_ref, w_in_ref, w_pool_mix_ref, pool_scale_ref, lb_ref,
                              hgrn_norm_ref, w_pool_up_ref, w_hgrn_up_ref, w_out_ref, x1_ref,
                              u_scr, z_scr, o_scr, s_scr, chunk, pos8, masks) for s in range(n_seq)]
    n_stages = len(streams[0])
    for slot in range(n_stages + n_seq - 1):
        lists = []
        for s, stages in enumerate(streams):
            if 0 <= slot - s < n_stages:
                lists += stages[slot - s]
        for phase in _merge_evenly(lists):
            phase()

    @pl.when(t == pl.num_programs(1) - 1)
    def _():
        pool_out_ref[...] = u_scr[:, 0:POOL_HALO, :]
        hgrn_out_ref[...] = s_scr[...]


def _mixer_sample_kernel(x_ref, sp_ref, sh_ref, g_mix_ref, w_in_ref, w_pool_mix_ref, pool_scale_ref,
                         lb_ref, hgrn_norm_ref, w_pool_up_ref, w_hgrn_up_ref, w_out_ref,
                         x1_ref, pool_out_ref, hgrn_out_ref,
                         e_scr, z_scr, o_scr, qe_scr, ke_scr, kd_scr, dec_scr, *, seq):
    rows_n, d_model = x_ref.shape
    buf, tb, pw = sp_ref.shape
    hw = o_scr.shape[1]
    n_heads = hw // LANES

    x = x_ref[...]
    h = _bf(_rms(x, g_mix_ref[...]))

    u = _dot(h, w_in_ref[:, 0:pw])
    pooled = []
    for g, w in enumerate(POOL_WINDOWS):
        ln = slice(g * LANES, (g + 1) * LANES)
        e_scr[g] = u[:, ln]
        ext = [sp_ref[e, :, ln] for e in range(buf)]
        ext += [e_scr[g, pl.ds(tt, tb, stride=seq), :] for tt in range(seq)]
        for e in range(buf):
            pool_out_ref[e, :, ln] = ext[seq + e]
        for tt in range(seq):
            acc = ext[buf + tt]
            for j in range(1, w):
                acc = acc + ext[buf + tt - j]
            e_scr[g, pl.ds(tt, tb, stride=seq), :] = acc * (1.0 / w) - ext[buf + tt]
        pooled.append(e_scr[g])
    col_ga = pw + 4 * hw
    ya = _pool_project(pooled, w_pool_mix_ref, pool_scale_ref, w_pool_up_ref)

    z_scr[...] = _dot(h, w_in_ref[:, pw:pw + 4 * hw])
    lb = _forget_lower_bound(lb_ref)
    pos8 = lax.broadcasted_iota(jnp.int32, (1, SUBLANES, LANES), 1)
    for hh in range(n_heads):
        ln = slice(hh * LANES, (hh + 1) * LANES)
        zq = z_scr[:, hh * LANES:(hh + 1) * LANES]
        zf = z_scr[:, hw + hh * LANES:hw + (hh + 1) * LANES]
        lbh = lb[:, ln]
        fg = lbh + (1.0 - lbh) * jax.nn.sigmoid(zf)
        k = 1.0 - fg
        q = _silu(zq)
        y = _scan_rows8(jnp.log(fg), pos8)
        tot = _last_row(y)
        cs = y.reshape(rows_n, LANES)
        qe_scr[:, ln] = q * jnp.exp(cs)
        ke_scr[:, ln] = k * jnp.exp(-cs)
        kd_scr[:, ln] = k * jnp.exp((tot - y).reshape(rows_n, LANES))
        dec_scr[:, ln] = jnp.exp(tot).reshape(rows_n, LANES)

    ti = lax.broadcasted_iota(jnp.int32, (seq, seq), 0)
    si = lax.broadcasted_iota(jnp.int32, (seq, seq), 1)
    causal = si <= ti

    def seq_body(b, carry):
        r0 = pl.multiple_of(b * seq, seq)
        rows = pl.ds(r0, seq)
        for hh in range(n_heads):
            ln = slice(hh * LANES, (hh + 1) * LANES)
            qe = qe_scr[rows, ln]
            v = z_scr[rows, 2 * hw + hh * LANES:2 * hw + (hh + 1) * LANES]
            a = lax.dot_general(_bf(qe), _bf(ke_scr[rows, ln]), _NT, preferred_element_type=F32)
            o, s_new = _finish_head(jnp.where(causal, a, 0.0), v, sh_ref[b, hh], qe,
                                    kd_scr[rows, ln], dec_scr[pl.ds(r0, 1), ln])
            hgrn_out_ref[b, hh] = s_new
            o_scr[rows, ln] = o
        return carry

    lax.fori_loop(0, tb, seq_body, 0, unroll=4)

    gates = _dot(h, w_in_ref[:, col_ga:col_ga + 2 * d_model])
    x1_ref[...] = _gated_merge_out(x, ya, gates[:, :d_model], gates[:, d_model:], o_scr[...],
                                   z_scr[:, 3 * hw:4 * hw], hgrn_norm_ref, w_hgrn_up_ref, w_out_ref,
                                   n_heads)


def _ffn_ple_kernel(xp_ref, xs_ref, pp_ref, ps_ref, g_ffn_ref, w_gate_ref, w_up_ref, w_down_ref,
                    g_ple_ref, w_ple_gate_ref, w_ple_proj_ref, g_final_ref, yp_ref, ys_ref, *, n_prompt):
    def tile(x_ref, p_ref, y_ref):
        x = x_ref[...]
        h2 = _bf(_rms(x, g_ffn_ref[...]))
        act = _silu(_dot(h2, w_gate_ref[...])) * _dot(h2, w_up_ref[...])
        x = x + _dot(_bf(act), w_down_ref[...])
        h3 = _bf(_rms(x, g_ple_ref[...]))
        gate = jax.nn.sigmoid(_dot(h3, w_ple_gate_ref[...]))
        emb = _dot(_bf(p_ref[...]), w_ple_proj_ref[...])
        x = x + gate * emb
        y_ref[...] = _rms(x, g_final_ref[...])

    i = pl.program_id(0)

    @pl.when(i < n_prompt)
    def _():
        tile(xp_ref, pp_ref, yp_ref)

    @pl.when(i >= n_prompt)
    def _():
        tile(xs_ref, ps_ref, ys_ref)


def _whole(_):
    return pl.BlockSpec(memory_space=pltpu.VMEM)


def _mixer_prompt(x, weights, to_cast, *, tile, chunk):
    bsz, seq, d_model = x.shape
    pw = weights[3].shape[1]
    hw = weights[5].shape[1]
    n_heads = hw // LANES
    ns = PROMPT_STREAMS
    assert seq % tile == 0 and tile % chunk == 0 and chunk % SUB == 0 and bsz % ns == 0
    nt = seq // tile
    n_steps = (bsz // ns) * nt
    slabs = []
    for w in to_cast:
        rep = 1 if w.shape[0] % (n_steps * 2 * SUBLANES) == 0 else 2
        rows = w.shape[0] * rep // n_steps
        assert rows * n_steps == w.shape[0] * rep and rows % (2 * SUBLANES) == 0
        slabs.append(pl.BlockSpec((rows, w.shape[1]), lambda b, t, rep=rep: ((b * nt + t) // rep, 0)))
    outs = pl.pallas_call(
        functools.partial(_mixer_prompt_kernel, chunk=chunk, n_cast=len(to_cast)),
        grid=(bsz // ns, nt),
        in_specs=([pl.BlockSpec((ns, tile, d_model), lambda b, t: (b, t, 0))]
                  + [_whole(w) for w in weights] + slabs),
        out_specs=[
            pl.BlockSpec((ns, tile, d_model), lambda b, t: (b, t, 0)),
            pl.BlockSpec((ns, POOL_HALO, pw), lambda b, t: (b, 0, 0)),
            pl.BlockSpec((ns, n_heads, LANES, LANES), lambda b, t: (b, 0, 0, 0)),
        ] + slabs,
        out_shape=[
            jax.ShapeDtypeStruct((bsz, seq, d_model), F32),
            jax.ShapeDtypeStruct((bsz, POOL_HALO, pw), F32),
            jax.ShapeDtypeStruct((bsz, n_heads, LANES, LANES), F32),
        ] + [jax.ShapeDtypeStruct(w.shape, BF16) for w in to_cast],
        scratch_shapes=[
            pltpu.VMEM((ns, POOL_HALO + tile, pw), F32),
            pltpu.VMEM((ns, tile, 4 * hw), F32),
            pltpu.VMEM((ns, tile, hw), F32),
            pltpu.VMEM((ns, n_heads, LANES, LANES), F32),
        ],
        compiler_params=pltpu.CompilerParams(
            dimension_semantics=("parallel", "arbitrary"), vmem_limit_bytes=VMEM_LIMIT),
        name="mixer_prompt",
    )(x, *weights, *to_cast)
    return outs[0], outs[1], outs[2], outs[3:]


def _mixer_sample(x, state_pool, state_hgrn, weights, *, tile_b):
    bsz, seq, d_model = x.shape
    buf, _, pw = state_pool.shape
    _, n_heads, dk, dv = state_hgrn.shape
    hw = n_heads * dv
    assert bsz % tile_b == 0 and seq == SUBLANES and buf >= max(POOL_WINDOWS) - 1
    assert dk == LANES and dv == LANES
    rows = tile_b * seq
    xf = x.reshape(bsz * seq, d_model)
    return pl.pallas_call(
        functools.partial(_mixer_sample_kernel, seq=seq),
        grid=(bsz // tile_b,),
        in_specs=[
            pl.BlockSpec((rows, d_model), lambda i: (i, 0)),
            pl.BlockSpec((buf, tile_b, pw), lambda i: (0, i, 0)),
            pl.BlockSpec((tile_b, n_heads, dk, dv), lambda i: (i, 0, 0, 0)),
        ] + [_whole(w) for w in weights],
        out_specs=[
            pl.BlockSpec((rows, d_model), lambda i: (i, 0)),
            pl.BlockSpec((buf, tile_b, pw), lambda i: (0, i, 0)),
            pl.BlockSpec((tile_b, n_heads, dk, dv), lambda i: (i, 0, 0, 0)),
        ],
        out_shape=[
            jax.ShapeDtypeStruct((bsz * seq, d_model), F32),
            jax.ShapeDtypeStruct((buf, bsz, pw), F32),
            jax.ShapeDtypeStruct((bsz, n_heads, dk, dv), F32),
        ],
        scratch_shapes=[
            pltpu.VMEM((pw // LANES, rows, LANES), F32),
            pltpu.VMEM((rows, 4 * hw), F32),
            pltpu.VMEM((rows, hw), F32),
            pltpu.VMEM((rows, hw), F32),
            pltpu.VMEM((rows, hw), F32),
            pltpu.VMEM((rows, hw), F32),
            pltpu.VMEM((rows, hw), F32),
        ],
        compiler_params=pltpu.CompilerParams(
            dimension_semantics=("parallel",), vmem_limit_bytes=VMEM_LIMIT),
        name="mixer_sample",
    )(xf, state_pool, state_hgrn, *weights)


def _ffn_ple(x_p, x_s, p_p, p_s, weights, *, tile):
    (n_p, d_model), n_s = x_p.shape, x_s.shape[0]
    p_dim = p_p.shape[1]
    assert n_p % tile == 0 and n_s % tile == 0
    tp, ts = n_p // tile, n_s // tile

    def prompt_idx(i):
        return (jnp.minimum(i, tp - 1), 0)

    def sample_idx(i):
        return (jnp.maximum(i - tp, 0), 0)

    return pl.pallas_call(
        functools.partial(_ffn_ple_kernel, n_prompt=tp),
        grid=(tp + ts,),
        in_specs=[
            pl.BlockSpec((tile, d_model), prompt_idx),
            pl.BlockSpec((tile, d_model), sample_idx),
            pl.BlockSpec((tile, p_dim), prompt_idx),
            pl.BlockSpec((tile, p_dim), sample_idx),
        ] + [_whole(w) for w in weights],
        out_specs=[
            pl.BlockSpec((tile, d_model), prompt_idx),
            pl.BlockSpec((tile, d_model), sample_idx),
        ],
        out_shape=[
            jax.ShapeDtypeStruct((n_p, d_model), F32),
            jax.ShapeDtypeStruct((n_s, d_model), F32),
        ],
        compiler_params=pltpu.CompilerParams(
            dimension_semantics=("arbitrary",), vmem_limit_bytes=VMEM_LIMIT),
        name="ffn_ple",
    )(x_p, x_s, p_p, p_s, *weights)


def kernel(x_prompt, x_sample, state_pool, state_hgrn, p_prompt, p_sample, g_mix, w_in, w_pool_mix, pool_scale, hgrn_lb, hgrn_norm, w_pool_up, w_hgrn_up, w_out, g_ffn, w_ffn_gate, w_ffn_up, w_ffn_down, g_ple, w_ple_gate, w_ple_proj, g_final):
    depth = w_in.shape[0]
    assert depth == 1 and hgrn_lb.shape[0] == 2
    bsz, seq, d_model = x_prompt.shape
    dbsz, dseq, _ = x_sample.shape
    buf = state_pool.shape[2]

    mixer_w = (g_mix, _bf(w_in[0]), _bf(w_pool_mix[0]), pool_scale, hgrn_lb, hgrn_norm,
               _bf(w_pool_up[0]), _bf(w_hgrn_up[0]), _bf(w_out[0]))
    x1_p, pool_p, hgrn_p, (wg, wu, wd, wpg, wpp) = _mixer_prompt(
        x_prompt, mixer_w, (w_ffn_gate[0], w_ffn_up[0], w_ffn_down[0], w_ple_gate[0], w_ple_proj[0]),
        tile=512, chunk=128)
    ffn_w = (g_ffn, wg, wu, wd, g_ple, wpg, wpp, g_final.reshape(1, d_model))
    x1_s, pool_s, hgrn_s = _mixer_sample(x_sample, jnp.swapaxes(state_pool[0], 0, 1), state_hgrn[0],
                                         mixer_w, tile_b=16)

    y_p, y_s = _ffn_ple(x1_p.reshape(bsz * seq, d_model), x1_s,
                        p_prompt[0].reshape(bsz * seq, -1), p_sample[0].reshape(dbsz * dseq, -1),
                        ffn_w, tile=512)

    return (y_p.reshape(bsz, seq, d_model), y_s.reshape(dbsz, dseq, d_model),
            pool_p[None, :, POOL_HALO - buf:, :], hgrn_p[None],
            jnp.swapaxes(pool_s, 0, 1)[None], hgrn_s[None])
```

```python
import functools

import jax
import jax.numpy as jnp
from jax import lax
from jax.experimental import pallas as pl
from jax.experimental.pallas import tpu as pltpu

F32 = jnp.float32
BF16 = jnp.bfloat16

EPS = 1e-6
POOL_WINDOWS = (2, 4, 8, 16)
POOL_HALO = 16
SUB = 16
LANES = 128
SUBLANES = 8
VMEM_LIMIT = 56 * 1024 * 1024
GATE_PIECE = 256
POOL_ROWS = 64
FFN_ROW_GROUPS = 2
FFN_PIECE = 768
PROMPT_STREAMS = 1

_NT = (((1,), (1,)), ((), ()))
_TN = (((0,), (0,)), ((), ()))


def _bf(x):
    return x.astype(BF16)


def _dot(a, b):
    return jnp.dot(a, b, preferred_element_type=F32)


def _rms(x, gain):
    ms = jnp.mean(x * x, axis=-1, keepdims=True)
    return x * lax.rsqrt(ms + EPS) * gain


def _silu(x):
    return x * jax.nn.sigmoid(x)


def _forget_lower_bound(lb_ref):
    a0 = lb_ref[0:1, :]
    a1 = lb_ref[1:2, :]
    m = jnp.maximum(a0, a1)
    e0 = jnp.exp(a0 - m)
    e1 = jnp.exp(a1 - m)
    return e0 / (e0 + e1)


def _scan_rows8(x, pos8):
    y = x.reshape(x.shape[0] // SUBLANES, SUBLANES, x.shape[1])
    s = 1
    while s < SUBLANES:
        y = y + jnp.where(pos8 >= s, pltpu.roll(y, s, axis=1), 0.0)
        s *= 2
    return y


def _last_row(y):
    return jnp.broadcast_to(y[..., SUBLANES - 1:SUBLANES, :], y.shape)


def _finish_head(a, v, s_prev, q_big, k_big, dec_row):
    vb = _bf(v)
    if a.shape[1] % LANES == 0:
        o = _dot(jnp.concatenate([_bf(a), _bf(q_big)], axis=1),
                 jnp.concatenate([vb, _bf(s_prev)], axis=0))
    else:
        o = _dot(_bf(a), vb) + _dot(_bf(q_big), _bf(s_prev))
    ds = lax.dot_general(_bf(k_big), vb, _TN, preferred_element_type=F32)
    dk = s_prev.shape[0]
    dec = jnp.transpose(jnp.broadcast_to(dec_row, (dk, dk)))
    return o, dec * s_prev + ds


def _scale_blocks(x, factors):
    out = []
    for i, f in enumerate(factors):
        blk = x[i * SUB:(i + 1) * SUB]
        out.append(blk if f is None else blk * jnp.concatenate([f] * (SUB // SUBLANES), axis=0))
    return jnp.concatenate(out, axis=0)


def _chunk_front(q, k, logf, pos8):
    n = q.shape[0]
    nb = n // SUB
    y = _scan_rows8(logf, pos8)
    cs_blk, tots = [], []
    for i in range(nb):
        lo = y[2 * i]
        hi = y[2 * i + 1] + _last_row(y[2 * i])
        cs_blk += [lo, hi]
        tots.append(_last_row(hi))
    cs = jnp.concatenate(cs_blk, axis=0)
    rs = jnp.concatenate([t for t in tots for _ in range(SUB // SUBLANES)], axis=0) - cs
    ecs = jnp.exp(cs)
    qe = q * ecs
    ke = k * jnp.exp(-cs)
    kd = k * jnp.exp(rs)

    def span(lo, hi):
        acc = None
        for m in range(lo, hi):
            acc = tots[m] if acc is None else acc + tots[m]
        return acc

    def expo(t):
        return None if t is None else jnp.exp(t)

    levels = []
    w = 2
    while w * SUB < n:
        qf = [expo(span((i // w) * w, i)) if (i // w) % 2 == 1 else None for i in range(nb)]
        kf = [expo(span(i + 1, (i // w + 1) * w)) if (i // w) % 2 == 0 else None for i in range(nb)]
        levels.append((_bf(_scale_blocks(qe, qf)), _bf(_scale_blocks(kd, kf))))
        w *= 2
    return {
        "q01": _bf(qe), "k01": _bf(jnp.concatenate([ke, kd], axis=0)),
        "levels": levels,
        "q_big": _bf(_scale_blocks(qe, [expo(span(0, i)) for i in range(nb)])),
        "k_big": _bf(_scale_blocks(kd, [expo(span(i + 1, nb)) for i in range(nb)])),
        "dec_row": jnp.exp(span(0, nb))[0:1, :],
    }


def _chunk_back(fr, v, s_prev, masks):
    n = fr["q01"].shape[0]
    a01 = lax.dot_general(fr["q01"], fr["k01"], _NT, preferred_element_type=F32)
    a = jnp.where(masks[0], a01[:, :n], jnp.where(masks[1], a01[:, n:], 0.0))
    for lvl, (q_l, k_l) in enumerate(fr["levels"]):
        a_l = lax.dot_general(q_l, k_l, _NT, preferred_element_type=F32)
        a = jnp.where(masks[lvl + 2], a_l, a)
    return _finish_head(a, v, s_prev, fr["q_big"], fr["k_big"], fr["dec_row"])


def _gated_merge_out(x, ya, ga, gb, o_raw, zg, hgrn_norm_ref, w_hgrn_up_ref, w_out_ref, n_heads):
    pieces = []
    for hh in range(n_heads):
        ln = slice(hh * LANES, (hh + 1) * LANES)
        oh = o_raw[:, ln]
        oh = oh * lax.rsqrt(jnp.mean(oh * oh, axis=-1, keepdims=True) + EPS)
        pieces.append(oh * hgrn_norm_ref[:, ln])
    o = jnp.concatenate(pieces, axis=-1) * _silu(zg)
    yb = _dot(_bf(o), w_hgrn_up_ref[...])
    merged = jax.nn.sigmoid(ga) * ya + jax.nn.sigmoid(gb) * yb
    return x + _dot(_bf(merged), w_out_ref[...])


def _pool_project(pooled, w_pool_mix_ref, pool_scale_ref, w_pool_up_ref):
    pieces = [_dot(_bf(pg), w_pool_mix_ref[g]) for g, pg in enumerate(pooled)]
    pool_out = jnp.concatenate(pieces, axis=-1) * pool_scale_ref[...]
    return _dot(_bf(pool_out), w_pool_up_ref[...])


def _pool_branch(u, shifted, inv_cnt, w_pool_mix_ref, pool_scale_ref, w_pool_up_ref):
    pooled = []
    for g, w in enumerate(POOL_WINDOWS):
        ln = slice(g * LANES, (g + 1) * LANES)
        ug = u[:, ln]
        acc = ug
        for j in range(1, w):
            acc = acc + shifted(j, ln)
        pooled.append(acc * inv_cnt(w) - ug)
    return _pool_project(pooled, w_pool_mix_ref, pool_scale_ref, w_pool_up_ref)


def _prompt_stream(s, t, x_ref, g_mix_ref, w_in_ref, w_pool_mix_ref, pool_scale_ref, lb_ref,
                   hgrn_norm_ref, w_pool_up_ref, w_hgrn_up_ref, w_out_ref, x1_ref,
                   u_scr, z_scr, o_scr, s_scr, chunk, pos8, masks):
    _, tm, d_model = x_ref.shape
    pw = u_scr.shape[2]
    hw = o_scr.shape[2]
    n_heads = hw // LANES
    col_ga = pw + 4 * hw
    st = {"gates": [], "pooled": [None] * len(POOL_WINDOWS)}

    def norm():
        st["h"] = _bf(_rms(x_ref[s], g_mix_ref[...]))
        st["lb"] = _forget_lower_bound(lb_ref)

    def proj_u():
        u_scr[s, POOL_HALO:POOL_HALO + tm, :] = _dot(st["h"], w_in_ref[:, 0:pw])

    def proj_z(j):
        def run():
            z_scr[s, :, j * hw:(j + 1) * hw] = _dot(st["h"], w_in_ref[:, pw + j * hw:pw + (j + 1) * hw])
        return run

    def pool_group(g):
        def run():
            w = POOL_WINDOWS[g]
            ln = slice(g * LANES, (g + 1) * LANES)
            out = []
            for r0 in range(0, tm, POOL_ROWS):
                seen = lax.broadcasted_iota(jnp.int32, (POOL_ROWS, 1), 0) + (t * tm + r0 + 1)
                ug = u_scr[s, POOL_HALO + r0:POOL_HALO + r0 + POOL_ROWS, ln]
                acc = ug
                for j in range(1, w):
                    acc = acc + u_scr[s, pl.ds(POOL_HALO + r0 - j, POOL_ROWS), ln]
                out.append(acc * (1.0 / jnp.minimum(seen, w).astype(F32)) - ug)
            st["pooled"][g] = jnp.concatenate(out, axis=0)
        return run

    def pool():
        st["ya"] = _pool_project(st["pooled"], w_pool_mix_ref, pool_scale_ref, w_pool_up_ref)
        u_scr[s, 0:POOL_HALO, :] = u_scr[s, tm:tm + POOL_HALO, :]

    steps = [(c, hh) for c in range(tm // chunk) for hh in range(n_heads)]

    def front(k):
        c, hh = steps[k]

        def run():
            rows = slice(c * chunk, (c + 1) * chunk)
            zq = z_scr[s, rows, hh * LANES:(hh + 1) * LANES]
            zf = z_scr[s, rows, hw + hh * LANES:hw + (hh + 1) * LANES]
            lbh = st["lb"][:, hh * LANES:(hh + 1) * LANES]
            fg = lbh + (1.0 - lbh) * jax.nn.sigmoid(zf)
            st["front"][k] = _chunk_front(_silu(zq), 1.0 - fg, jnp.log(fg), pos8)
        return run

    def back(k):
        c, hh = steps[k]

        def run():
            rows = slice(c * chunk, (c + 1) * chunk)
            v = z_scr[s, rows, 2 * hw + hh * LANES:2 * hw + (hh + 1) * LANES]
            o, s_new = _chunk_back(st["front"][k], v, s_scr[s, hh], masks)
            s_scr[s, hh] = s_new
            o_scr[s, rows, hh * LANES:(hh + 1) * LANES] = o
        return run

    def gate(c0):
        def run():
            st["gates"].append(_dot(st["h"], w_in_ref[:, c0:c0 + GATE_PIECE]))
        return run

    def tail():
        gates = jnp.concatenate(st["gates"], axis=1)
        x1_ref[s] = _gated_merge_out(x_ref[s], st["ya"], gates[:, :d_model], gates[:, d_model:],
                                     o_scr[s], z_scr[s, :, 3 * hw:4 * hw], hgrn_norm_ref,
                                     w_hgrn_up_ref, w_out_ref, n_heads)

    st["front"] = [None] * len(steps)
    gate_cols = [(col_ga + j * GATE_PIECE) for j in range(2 * d_model // GATE_PIECE)]
    return [
        [[norm]],
        [[proj_u] + [proj_z(j) for j in range(4)]],
        [[front(k) for k in range(len(steps))], [pool_group(g) for g in reversed(range(len(POOL_WINDOWS)))]],
        [[back(k) for k in range(len(steps))], [gate(c0) for c0 in gate_cols] + [pool]],
        [[tail]],
    ]


def _merge_evenly(lists):
    keyed = [((i + 0.5) / len(lst), n, fn) for n, lst in enumerate(lists) for i, fn in enumerate(lst)]
    return [fn for _, _, fn in sorted(keyed, key=lambda e: e[:2])]


def _emit_staggered(streams):
    n_stages = len(streams[0])
    for slot in range(n_stages + len(streams) - 1):
        lists = []
        for s, stages in enumerate(streams):
            if 0 <= slot - s < n_stages:
                lists += stages[slot - s]
        for phase in _merge_evenly(lists):
            phase()


def _mixer_prompt_kernel(x_ref, g_mix_ref, w_in_ref, w_pool_mix_ref, pool_scale_ref, lb_ref,
                         hgrn_norm_ref, w_pool_up_ref, w_hgrn_up_ref, w_out_ref, *rest, chunk, n_cast):
    cast_in = rest[:n_cast]
    x1_ref, pool_out_ref, hgrn_out_ref = rest[n_cast:n_cast + 3]
    cast_out = rest[n_cast + 3:2 * n_cast + 3]
    u_scr, z_scr, o_scr, s_scr = rest[2 * n_cast + 3:]
    n_seq = x_ref.shape[0]
    t = pl.program_id(1)

    for src, dst in zip(cast_in, cast_out):
        dst[...] = _bf(src[...])

    @pl.when(t == 0)
    def _():
        u_scr[:, 0:POOL_HALO, :] = jnp.zeros((n_seq, POOL_HALO, u_scr.shape[2]), F32)
        s_scr[...] = jnp.zeros(s_scr.shape, F32)

    pos8 = lax.broadcasted_iota(jnp.int32, (1, SUBLANES, LANES), 1)
    ti = lax.broadcasted_iota(jnp.int32, (chunk, chunk), 0)
    si = lax.broadcasted_iota(jnp.int32, (chunk, chunk), 1)
    masks = [((ti // SUB) == (si // SUB)) & (si <= ti)]
    b = SUB
    while b < chunk:
        masks.append(((ti // (2 * b)) == (si // (2 * b))) & ((ti & b) != 0) & ((si & b) == 0))
        b *= 2

    streams = [_prompt_stream(s, t, x_ref, g_mix_ref, w_in_ref, w_pool_mix_ref, pool_scale_ref, lb_ref,
                              hgrn_norm_ref, w_pool_up_ref, w_hgrn_up_ref, w_out_ref, x1_ref,
                              u_scr, z_scr, o_scr, s_scr, chunk, pos8, masks) for s in range(n_seq)]
    _emit_staggered(streams)

    @pl.when(t == pl.num_programs(1) - 1)
    def _():
        pool_out_ref[...] = u_scr[:, 0:POOL_HALO, :]
        hgrn_out_ref[...] = s_scr[...]


def _mixer_sample_kernel(x_ref, sp_ref, sh_ref, g_mix_ref, w_in_ref, w_pool_mix_ref, pool_scale_ref,
                         lb_ref, hgrn_norm_ref, w_pool_up_ref, w_hgrn_up_ref, w_out_ref,
                         x1_ref, pool_out_ref, hgrn_out_ref,
                         e_scr, z_scr, o_scr, qe_scr, ke_scr, kd_scr, dec_scr, *, seq):
    rows_n, d_model = x_ref.shape
    buf, tb, pw = sp_ref.shape
    hw = o_scr.shape[1]
    n_heads = hw // LANES

    x = x_ref[...]
    h = _bf(_rms(x, g_mix_ref[...]))

    u = _dot(h, w_in_ref[:, 0:pw])
    pooled = []
    for g, w in enumerate(POOL_WINDOWS):
        ln = slice(g * LANES, (g + 1) * LANES)
        e_scr[g] = u[:, ln]
        ext = [sp_ref[e, :, ln] for e in range(buf)]
        ext += [e_scr[g, pl.ds(tt, tb, stride=seq), :] for tt in range(seq)]
        for e in range(buf):
            pool_out_ref[e, :, ln] = ext[seq + e]
        for tt in range(seq):
            acc = ext[buf + tt]
            for j in range(1, w):
                acc = acc + ext[buf + tt - j]
            e_scr[g, pl.ds(tt, tb, stride=seq), :] = acc * (1.0 / w) - ext[buf + tt]
        pooled.append(e_scr[g])
    col_ga = pw + 4 * hw
    ya = _pool_project(pooled, w_pool_mix_ref, pool_scale_ref, w_pool_up_ref)

    z_scr[...] = _dot(h, w_in_ref[:, pw:pw + 4 * hw])
    lb = _forget_lower_bound(lb_ref)
    pos8 = lax.broadcasted_iota(jnp.int32, (1, SUBLANES, LANES), 1)
    for hh in range(n_heads):
        ln = slice(hh * LANES, (hh + 1) * LANES)
        zq = z_scr[:, hh * LANES:(hh + 1) * LANES]
        zf = z_scr[:, hw + hh * LANES:hw + (hh + 1) * LANES]
        lbh = lb[:, ln]
        fg = lbh + (1.0 - lbh) * jax.nn.sigmoid(zf)
        k = 1.0 - fg
        q = _silu(zq)
        y = _scan_rows8(jnp.log(fg), pos8)
        tot = _last_row(y)
        cs = y.reshape(rows_n, LANES)
        qe_scr[:, ln] = q * jnp.exp(cs)
        ke_scr[:, ln] = k * jnp.exp(-cs)
        kd_scr[:, ln] = k * jnp.exp((tot - y).reshape(rows_n, LANES))
        dec_scr[:, ln] = jnp.exp(tot).reshape(rows_n, LANES)

    ti = lax.broadcasted_iota(jnp.int32, (seq, seq), 0)
    si = lax.broadcasted_iota(jnp.int32, (seq, seq), 1)
    causal = si <= ti

    def seq_body(b, carry):
        r0 = pl.multiple_of(b * seq, seq)
        rows = pl.ds(r0, seq)
        for hh in range(n_heads):
            ln = slice(hh * LANES, (hh + 1) * LANES)
            qe = qe_scr[rows, ln]
            v = z_scr[rows, 2 * hw + hh * LANES:2 * hw + (hh + 1) * LANES]
            a = lax.dot_general(_bf(qe), _bf(ke_scr[rows, ln]), _NT, preferred_element_type=F32)
            o, s_new = _finish_head(jnp.where(causal, a, 0.0), v, sh_ref[b, hh], qe,
                                    kd_scr[rows, ln], dec_scr[pl.ds(r0, 1), ln])
            hgrn_out_ref[b, hh] = s_new
            o_scr[rows, ln] = o
        return carry

    lax.fori_loop(0, tb, seq_body, 0, unroll=4)

    gates = _dot(h, w_in_ref[:, col_ga:col_ga + 2 * d_model])
    x1_ref[...] = _gated_merge_out(x, ya, gates[:, :d_model], gates[:, d_model:], o_scr[...],
                                   z_scr[:, 3 * hw:4 * hw], hgrn_norm_ref, w_hgrn_up_ref, w_out_ref,
                                   n_heads)


def _ffn_ple_kernel(xp_ref, xs_ref, pp_ref, ps_ref, g_ffn_ref, w_gate_ref, w_up_ref, w_down_ref,
                    g_ple_ref, w_ple_gate_ref, w_ple_proj_ref, g_final_ref, yp_ref, ys_ref, *, n_prompt):
    d_ff = w_gate_ref.shape[1]
    ff_cuts = list(range(0, d_ff, FFN_PIECE)) + [d_ff]

    def row_group(x_ref, p_ref, y_ref, rows):
        st = {"act": []}

        def norm():
            st["x"] = x_ref[rows, :]
            st["h2"] = _bf(_rms(st["x"], g_ffn_ref[...]))

        def ff(c0, c1):
            def run():
                g = _dot(st["h2"], w_gate_ref[:, c0:c1])
                st["act"].append(_bf(_silu(g) * _dot(st["h2"], w_up_ref[:, c0:c1])))
            return run

        def down():
            x = st["x"] + _dot(jnp.concatenate(st["act"], axis=1), w_down_ref[...])
            st["x"] = x
            st["h3"] = _bf(_rms(x, g_ple_ref[...]))

        def ple():
            gate = jax.nn.sigmoid(_dot(st["h3"], w_ple_gate_ref[...]))
            emb = _dot(_bf(p_ref[rows, :]), w_ple_proj_ref[...])
            y_ref[rows, :] = _rms(st["x"] + gate * emb, g_final_ref[...])

        return [[[norm]], [[ff(c0, c1) for c0, c1 in zip(ff_cuts, ff_cuts[1:])]], [[down]], [[ple]]]

    def tile(x_ref, p_ref, y_ref):
        n = x_ref.shape[0] // FFN_ROW_GROUPS
        _emit_staggered([row_group(x_ref, p_ref, y_ref, slice(r * n, (r + 1) * n))
                         for r in range(FFN_ROW_GROUPS)])

    i = pl.program_id(0)

    @pl.when(i < n_prompt)
    def _():
        tile(xp_ref, pp_ref, yp_ref)

    @pl.when(i >= n_prompt)
    def _():
        tile(xs_ref, ps_ref, ys_ref)


def _whole(_):
    return pl.BlockSpec(memory_space=pltpu.VMEM)


def _mixer_prompt(x, weights, to_cast, *, tile, chunk):
    bsz, seq, d_model = x.shape
    pw = weights[3].shape[1]
    hw = weights[5].shape[1]
    n_heads = hw // LANES
    ns = PROMPT_STREAMS
    assert seq % tile == 0 and tile % chunk == 0 and chunk % SUB == 0 and bsz % ns == 0
    nt = seq // tile
    n_steps = (bsz // ns) * nt
    slabs = []
    for w in to_cast:
        rep = 1 if w.shape[0] % (n_steps * 2 * SUBLANES) == 0 else 2
        rows = w.shape[0] * rep // n_steps
        assert rows * n_steps == w.shape[0] * rep and rows % (2 * SUBLANES) == 0
        slabs.append(pl.BlockSpec((rows, w.shape[1]), lambda b, t, rep=rep: ((b * nt + t) // rep, 0)))
    outs = pl.pallas_call(
        functools.partial(_mixer_prompt_kernel, chunk=chunk, n_cast=len(to_cast)),
        grid=(bsz // ns, nt),
        in_specs=([pl.BlockSpec((ns, tile, d_model), lambda b, t: (b, t, 0))]
                  + [_whole(w) for w in weights] + slabs),
        out_specs=[
            pl.BlockSpec((ns, tile, d_model), lambda b, t: (b, t, 0)),
            pl.BlockSpec((ns, POOL_HALO, pw), lambda b, t: (b, 0, 0)),
            pl.BlockSpec((ns, n_heads, LANES, LANES), lambda b, t: (b, 0, 0, 0)),
        ] + slabs,
        out_shape=[
            jax.ShapeDtypeStruct((bsz, seq, d_model), F32),
            jax.ShapeDtypeStruct((bsz, POOL_HALO, pw), F32),
            jax.ShapeDtypeStruct((bsz, n_heads, LANES, LANES), F32),
        ] + [jax.ShapeDtypeStruct(w.shape, BF16) for w in to_cast],
        scratch_shapes=[
            pltpu.VMEM((ns, POOL_HALO + tile, pw), F32),
            pltpu.VMEM((ns, tile, 4 * hw), F32),
            pltpu.VMEM((ns, tile, hw), F32),
            pltpu.VMEM((ns, n_heads, LANES, LANES), F32),
        ],
        compiler_params=pltpu.CompilerParams(
            dimension_semantics=("parallel", "arbitrary"), vmem_limit_bytes=VMEM_LIMIT),
        name="mixer_prompt",
    )(x, *weights, *to_cast)
    return outs[0], outs[1], outs[2], outs[3:]


def _mixer_sample(x, state_pool, state_hgrn, weights, *, tile_b):
    bsz, seq, d_model = x.shape
    buf, _, pw = state_pool.shape
    _, n_heads, dk, dv = state_hgrn.shape
    hw = n_heads * dv
    assert bsz % tile_b == 0 and seq == SUBLANES and buf >= max(POOL_WINDOWS) - 1
    assert dk == LANES and dv == LANES
    rows = tile_b * seq
    xf = x.reshape(bsz * seq, d_model)
    return pl.pallas_call(
        functools.partial(_mixer_sample_kernel, seq=seq),
        grid=(bsz // tile_b,),
        in_specs=[
            pl.BlockSpec((rows, d_model), lambda i: (i, 0)),
            pl.BlockSpec((buf, tile_b, pw), lambda i: (0, i, 0)),
            pl.BlockSpec((tile_b, n_heads, dk, dv), lambda i: (i, 0, 0, 0)),
        ] + [_whole(w) for w in weights],
        out_specs=[
            pl.BlockSpec((rows, d_model), lambda i: (i, 0)),
            pl.BlockSpec((buf, tile_b, pw), lambda i: (0, i, 0)),
            pl.BlockSpec((tile_b, n_heads, dk, dv), lambda i: (i, 0, 0, 0)),
        ],
        out_shape=[
            jax.ShapeDtypeStruct((bsz * seq, d_model), F32),
            jax.ShapeDtypeStruct((buf, bsz, pw), F32),
            jax.ShapeDtypeStruct((bsz, n_heads, dk, dv), F32),
        ],
        scratch_shapes=[
            pltpu.VMEM((pw // LANES, rows, LANES), F32),
            pltpu.VMEM((rows, 4 * hw), F32),
            pltpu.VMEM((rows, hw), F32),
            pltpu.VMEM((rows, hw), F32),
            pltpu.VMEM((rows, hw), F32),
            pltpu.VMEM((rows, hw), F32),
            pltpu.VMEM((rows, hw), F32),
        ],
        compiler_params=pltpu.CompilerParams(
            dimension_semantics=("parallel",), vmem_limit_bytes=VMEM_LIMIT),
        name="mixer_sample",
    )(xf, state_pool, state_hgrn, *weights)


def _ffn_ple(x_p, x_s, p_p, p_s, weights, *, tile):
    (n_p, d_model), n_s = x_p.shape, x_s.shape[0]
    p_dim = p_p.shape[1]
    assert n_p % tile == 0 and n_s % tile == 0
    tp, ts = n_p // tile, n_s // tile

    def prompt_idx(i):
        return (jnp.minimum(i, tp - 1), 0)

    def sample_idx(i):
        return (jnp.maximum(i - tp, 0), 0)

    return pl.pallas_call(
        functools.partial(_ffn_ple_kernel, n_prompt=tp),
        grid=(tp + ts,),
        in_specs=[
            pl.BlockSpec((tile, d_model), prompt_idx),
            pl.BlockSpec((tile, d_model), sample_idx),
            pl.BlockSpec((tile, p_dim), prompt_idx),
            pl.BlockSpec((tile, p_dim), sample_idx),
        ] + [_whole(w) for w in weights],
        out_specs=[
            pl.BlockSpec((tile, d_model), prompt_idx),
            pl.BlockSpec((tile, d_model), sample_idx),
        ],
        out_shape=[
            jax.ShapeDtypeStruct((n_p, d_model), F32),
            jax.ShapeDtypeStruct((n_s, d_model), F32),
        ],
        compiler_params=pltpu.CompilerParams(
            dimension_semantics=("arbitrary",), vmem_limit_bytes=VMEM_LIMIT),
        name="ffn_ple",
    )(x_p, x_s, p_p, p_s, *weights)


def kernel(x_prompt, x_sample, state_pool, state_hgrn, p_prompt, p_sample, g_mix, w_in, w_pool_mix, pool_scale, hgrn_lb, hgrn_norm, w_pool_up, w_hgrn_up, w_out, g_ffn, w_ffn_gate, w_ffn_up, w_ffn_down, g_ple, w_ple_gate, w_ple_proj, g_final):
    depth = w_in.shape[0]
    assert depth == 1 and hgrn_lb.shape[0] == 2
    bsz, seq, d_model = x_prompt.shape
    dbsz, dseq, _ = x_sample.shape
    buf = state_pool.shape[2]

    mixer_w = (g_mix, _bf(w_in[0]), _bf(w_pool_mix[0]), pool_scale, hgrn_lb, hgrn_norm,
               _bf(w_pool_up[0]), _bf(w_hgrn_up[0]), _bf(w_out[0]))
    x1_p, pool_p, hgrn_p, (wg, wu, wd, wpg, wpp) = _mixer_prompt(
        x_prompt, mixer_w, (w_ffn_gate[0], w_ffn_up[0], w_ffn_down[0], w_ple_gate[0], w_ple_proj[0]),
        tile=512, chunk=128)
    ffn_w = (g_ffn, wg, wu, wd, g_ple, wpg, wpp, g_final.reshape(1, d_model))
    x1_s, pool_s, hgrn_s = _mixer_sample(x_sample, jnp.swapaxes(state_pool[0], 0, 1), state_hgrn[0],
                                         mixer_w, tile_b=16)

    y_p, y_s = _ffn_ple(x1_p.reshape(bsz * seq, d_model), x1_s,
                        p_prompt[0].reshape(bsz * seq, -1), p_sample[0].reshape(dbsz * dseq, -1),
                        ffn_w, tile=512)

    return (y_p.reshape(bsz, seq, d_model), y_s.reshape(dbsz, dseq, d_model),
            pool_p[None, :, POOL_HALO - buf:, :], hgrn_p[None],
            jnp.swapaxes(pool_s, 0, 1)[None], hgrn_s[None])
```

```python
import functools

import jax
import jax.numpy as jnp
from jax import lax
from jax.experimental import pallas as pl
from jax.experimental.pallas import tpu as pltpu

F32 = jnp.float32
BF16 = jnp.bfloat16

EPS = 1e-6
POOL_WINDOWS = (2, 4, 8, 16)
assert POOL_WINDOWS == tuple(2 << g for g in range(len(POOL_WINDOWS)))
POOL_HALO = 16
POOL_PAD = 8
POOL_TOP = POOL_PAD + POOL_HALO
SUB = 16
LANES = 128
SUBLANES = 8
VMEM_LIMIT = 56 * 1024 * 1024
GATE_PIECE = 256
FFN_ROW_GROUPS = 2
FFN_PIECE = 768
PROMPT_STREAMS = 1

_NT = (((1,), (1,)), ((), ()))
_TN = (((0,), (0,)), ((), ()))


def _bf(x):
    return x.astype(BF16)


def _dot(a, b):
    return jnp.dot(a, b, preferred_element_type=F32)


def _rms(x, gain):
    ms = jnp.mean(x * x, axis=-1, keepdims=True)
    return x * lax.rsqrt(ms + EPS) * gain


def _silu(x):
    return x * jax.nn.sigmoid(x)


def _forget_lower_bound(lb_ref):
    a0 = lb_ref[0:1, :]
    a1 = lb_ref[1:2, :]
    m = jnp.maximum(a0, a1)
    e0 = jnp.exp(a0 - m)
    e1 = jnp.exp(a1 - m)
    return e0 / (e0 + e1)


def _scan_rows8(x, pos8):
    y = x.reshape(x.shape[0] // SUBLANES, SUBLANES, x.shape[1])
    s = 1
    while s < SUBLANES:
        y = y + jnp.where(pos8 >= s, pltpu.roll(y, s, axis=1), 0.0)
        s *= 2
    return y


def _last_row(y):
    return jnp.broadcast_to(y[..., SUBLANES - 1:SUBLANES, :], y.shape)


def _finish_head(a, v, s_prev, q_big, k_big, dec_row):
    vb = _bf(v)
    if a.shape[1] % LANES == 0:
        o = _dot(jnp.concatenate([_bf(a), _bf(q_big)], axis=1),
                 jnp.concatenate([vb, _bf(s_prev)], axis=0))
    else:
        o = _dot(_bf(a), vb) + _dot(_bf(q_big), _bf(s_prev))
    ds = lax.dot_general(_bf(k_big), vb, _TN, preferred_element_type=F32)
    dk = s_prev.shape[0]
    dec = jnp.transpose(jnp.broadcast_to(dec_row, (dk, dk)))
    return o, dec * s_prev + ds


def _scale_blocks(x, factors):
    out = []
    for i, f in enumerate(factors):
        blk = x[i * SUB:(i + 1) * SUB]
        out.append(blk if f is None else blk * jnp.concatenate([f] * (SUB // SUBLANES), axis=0))
    return jnp.concatenate(out, axis=0)


def _chunk_front(q, k, logf, pos8):
    n = q.shape[0]
    nb = n // SUB
    y = _scan_rows8(logf, pos8)
    cs_blk, tots = [], []
    for i in range(nb):
        lo = y[2 * i]
        hi = y[2 * i + 1] + _last_row(y[2 * i])
        cs_blk += [lo, hi]
        tots.append(_last_row(hi))
    cs = jnp.concatenate(cs_blk, axis=0)
    rs = jnp.concatenate([t for t in tots for _ in range(SUB // SUBLANES)], axis=0) - cs
    ecs = jnp.exp(cs)
    qe = q * ecs
    ke = k * jnp.exp(-cs)
    kd = k * jnp.exp(rs)

    def span(lo, hi):
        acc = None
        for m in range(lo, hi):
            acc = tots[m] if acc is None else acc + tots[m]
        return acc

    def expo(t):
        return None if t is None else jnp.exp(t)

    levels = []
    w = 2
    while w * SUB < n:
        qf = [expo(span((i // w) * w, i)) if (i // w) % 2 == 1 else None for i in range(nb)]
        kf = [expo(span(i + 1, (i // w + 1) * w)) if (i // w) % 2 == 0 else None for i in range(nb)]
        levels.append((_bf(_scale_blocks(qe, qf)), _bf(_scale_blocks(kd, kf))))
        w *= 2
    return {
        "q01": _bf(qe), "k01": _bf(jnp.concatenate([ke, kd], axis=0)),
        "levels": levels,
        "q_big": _bf(_scale_blocks(qe, [expo(span(0, i)) for i in range(nb)])),
        "k_big": _bf(_scale_blocks(kd, [expo(span(i + 1, nb)) for i in range(nb)])),
        "dec_row": jnp.exp(span(0, nb))[0:1, :],
    }


def _chunk_back(fr, v, s_prev, masks):
    n = fr["q01"].shape[0]
    a01 = lax.dot_general(fr["q01"], fr["k01"], _NT, preferred_element_type=F32)
    a = jnp.where(masks[0], a01[:, :n], jnp.where(masks[1], a01[:, n:], 0.0))
    for lvl, (q_l, k_l) in enumerate(fr["levels"]):
        a_l = lax.dot_general(q_l, k_l, _NT, preferred_element_type=F32)
        a = jnp.where(masks[lvl + 2], a_l, a)
    return _finish_head(a, v, s_prev, fr["q_big"], fr["k_big"], fr["dec_row"])


def _gated_merge_out(x, ya, ga, gb, o_raw, zg, hgrn_norm_ref, w_hgrn_up_ref, w_out_ref, n_heads):
    pieces = []
    for hh in range(n_heads):
        ln = slice(hh * LANES, (hh + 1) * LANES)
        oh = o_raw[:, ln]
        oh = oh * lax.rsqrt(jnp.mean(oh * oh, axis=-1, keepdims=True) + EPS)
        pieces.append(oh * hgrn_norm_ref[:, ln])
    o = jnp.concatenate(pieces, axis=-1) * _silu(zg)
    yb = _dot(_bf(o), w_hgrn_up_ref[...])
    merged = jax.nn.sigmoid(ga) * ya + jax.nn.sigmoid(gb) * yb
    return x + _dot(_bf(merged), w_out_ref[...])


def _pool_project(pooled, w_pool_mix_ref, pool_scale_ref, w_pool_up_ref):
    pieces = [_dot(_bf(pg), w_pool_mix_ref[g]) for g, pg in enumerate(pooled)]
    pool_out = jnp.concatenate(pieces, axis=-1) * pool_scale_ref[...]
    return _dot(_bf(pool_out), w_pool_up_ref[...])


def _prompt_stream(s, t, x_ref, g_mix_ref, w_in_ref, w_pool_mix_ref, pool_scale_ref, lb_ref,
                   hgrn_norm_ref, w_pool_up_ref, w_hgrn_up_ref, w_out_ref, x1_ref,
                   u_scr, z_scr, o_scr, s_scr, chunk, pos8, masks):
    _, tm, d_model = x_ref.shape
    pw = u_scr.shape[3]
    hw = o_scr.shape[2]
    n_heads = hw // LANES
    col_ga = pw + 4 * hw
    st = {"gates": [], "pooled": [None] * len(POOL_WINDOWS)}

    def norm():
        st["h"] = _bf(_rms(x_ref[s], g_mix_ref[...]))
        st["lb"] = _forget_lower_bound(lb_ref)

    def proj_u():
        u_scr[s, 0, POOL_TOP:POOL_TOP + tm, :] = _dot(st["h"], w_in_ref[:, 0:pw])

    def proj_z(j):
        def run():
            z_scr[s, :, j * hw:(j + 1) * hw] = _dot(st["h"], w_in_ref[:, pw + j * hw:pw + (j + 1) * hw])
        return run

    def window_level(lv):
        def run():
            shift = 1 << lv
            lanes = slice(lv * LANES, pw)
            src = 0 if lv == 0 else 1 + (lv - 1) % 2
            rows_all = POOL_HALO + tm
            sums = (u_scr[s, src, POOL_PAD:POOL_PAD + rows_all, lanes]
                    + u_scr[s, src, pl.ds(POOL_PAD - shift, rows_all), lanes])
            if lanes.start + LANES < pw:
                u_scr[s, 1 + lv % 2, POOL_PAD:POOL_PAD + rows_all, lanes] = sums
            seen = lax.broadcasted_iota(jnp.int32, (tm, 1), 0) + (t * tm + 1)
            inv_cnt = 1.0 / jnp.minimum(seen, 2 * shift).astype(F32)
            ug = u_scr[s, 0, POOL_TOP:POOL_TOP + tm, lv * LANES:(lv + 1) * LANES]
            st["pooled"][lv] = sums[POOL_HALO:, 0:LANES] * inv_cnt - ug
        return run

    def pool():
        st["ya"] = _pool_project(st["pooled"], w_pool_mix_ref, pool_scale_ref, w_pool_up_ref)
        u_scr[s, 0, POOL_PAD:POOL_TOP, :] = u_scr[s, 0, POOL_PAD + tm:POOL_TOP + tm, :]

    steps = [(c, hh) for c in range(tm // chunk) for hh in range(n_heads)]

    def front(k):
        c, hh = steps[k]

        def run():
            rows = slice(c * chunk, (c + 1) * chunk)
            zq = z_scr[s, rows, hh * LANES:(hh + 1) * LANES]
            zf = z_scr[s, rows, hw + hh * LANES:hw + (hh + 1) * LANES]
            lbh = st["lb"][:, hh * LANES:(hh + 1) * LANES]
            fg = lbh + (1.0 - lbh) * jax.nn.sigmoid(zf)
            st["front"][k] = _chunk_front(_silu(zq), 1.0 - fg, jnp.log(fg), pos8)
        return run

    def back(k):
        c, hh = steps[k]

        def run():
            rows = slice(c * chunk, (c + 1) * chunk)
            v = z_scr[s, rows, 2 * hw + hh * LANES:2 * hw + (hh + 1) * LANES]
            o, s_new = _chunk_back(st["front"][k], v, s_scr[s, hh], masks)
            s_scr[s, hh] = s_new
            o_scr[s, rows, hh * LANES:(hh + 1) * LANES] = o
        return run

    def gate(c0):
        def run():
            st["gates"].append(_dot(st["h"], w_in_ref[:, c0:c0 + GATE_PIECE]))
        return run

    def tail():
        gates = jnp.concatenate(st["gates"], axis=1)
        x1_ref[s] = _gated_merge_out(x_ref[s], st["ya"], gates[:, :d_model], gates[:, d_model:],
                                     o_scr[s], z_scr[s, :, 3 * hw:4 * hw], hgrn_norm_ref,
                                     w_hgrn_up_ref, w_out_ref, n_heads)

    st["front"] = [None] * len(steps)
    gate_cols = [(col_ga + j * GATE_PIECE) for j in range(2 * d_model // GATE_PIECE)]
    return [
        [[norm]],
        [[proj_u] + [proj_z(j) for j in range(4)]],
        [[front(k) for k in range(len(steps))], [window_level(lv) for lv in range(len(POOL_WINDOWS))]],
        [[back(k) for k in range(len(steps))], [gate(c0) for c0 in gate_cols] + [pool]],
        [[tail]],
    ]


def _merge_evenly(lists):
    keyed = [((i + 0.5) / len(lst), n, fn) for n, lst in enumerate(lists) for i, fn in enumerate(lst)]
    return [fn for _, _, fn in sorted(keyed, key=lambda e: e[:2])]


def _emit_staggered(streams):
    n_stages = len(streams[0])
    for slot in range(n_stages + len(streams) - 1):
        lists = []
        for s, stages in enumerate(streams):
            if 0 <= slot - s < n_stages:
                lists += stages[slot - s]
        for phase in _merge_evenly(lists):
            phase()


def _mixer_prompt_kernel(x_ref, g_mix_ref, w_in_ref, w_pool_mix_ref, pool_scale_ref, lb_ref,
                         hgrn_norm_ref, w_pool_up_ref, w_hgrn_up_ref, w_out_ref, *rest, chunk, n_cast):
    cast_in = rest[:n_cast]
    x1_ref, pool_out_ref, hgrn_out_ref = rest[n_cast:n_cast + 3]
    cast_out = rest[n_cast + 3:2 * n_cast + 3]
    u_scr, z_scr, o_scr, s_scr = rest[2 * n_cast + 3:]
    n_seq = x_ref.shape[0]
    t = pl.program_id(1)

    for src, dst in zip(cast_in, cast_out):
        dst[...] = _bf(src[...])

    @pl.when(t == 0)
    def _():
        u_scr[:, :, 0:POOL_TOP, :] = jnp.zeros(u_scr.shape[:2] + (POOL_TOP, u_scr.shape[3]), F32)
        s_scr[...] = jnp.zeros(s_scr.shape, F32)

    pos8 = lax.broadcasted_iota(jnp.int32, (1, SUBLANES, LANES), 1)
    ti = lax.broadcasted_iota(jnp.int32, (chunk, chunk), 0)
    si = lax.broadcasted_iota(jnp.int32, (chunk, chunk), 1)
    masks = [((ti // SUB) == (si // SUB)) & (si <= ti)]
    b = SUB
    while b < chunk:
        masks.append(((ti // (2 * b)) == (si // (2 * b))) & ((ti & b) != 0) & ((si & b) == 0))
        b *= 2

    streams = [_prompt_stream(s, t, x_ref, g_mix_ref, w_in_ref, w_pool_mix_ref, pool_scale_ref, lb_ref,
                              hgrn_norm_ref, w_pool_up_ref, w_hgrn_up_ref, w_out_ref, x1_ref,
                              u_scr, z_scr, o_scr, s_scr, chunk, pos8, masks) for s in range(n_seq)]
    _emit_staggered(streams)

    @pl.when(t == pl.num_programs(1) - 1)
    def _():
        pool_out_ref[...] = u_scr[:, 0, POOL_PAD:POOL_TOP, :]
        hgrn_out_ref[...] = s_scr[...]


def _mixer_sample_kernel(x_ref, sp_ref, sh_ref, g_mix_ref, w_in_ref, w_pool_mix_ref, pool_scale_ref,
                         lb_ref, hgrn_norm_ref, w_pool_up_ref, w_hgrn_up_ref, w_out_ref,
                         x1_ref, pool_out_ref, hgrn_out_ref,
                         e_scr, z_scr, o_scr, qe_scr, ke_scr, kd_scr, dec_scr, *, seq):
    rows_n, d_model = x_ref.shape
    buf, tb, pw = sp_ref.shape
    hw = o_scr.shape[1]
    n_heads = hw // LANES

    x = x_ref[...]
    h = _bf(_rms(x, g_mix_ref[...]))

    u = _dot(h, w_in_ref[:, 0:pw])
    pooled = []
    for g, w in enumerate(POOL_WINDOWS):
        ln = slice(g * LANES, (g + 1) * LANES)
        e_scr[g] = u[:, ln]
        ext = [sp_ref[e, :, ln] for e in range(buf)]
        ext += [e_scr[g, pl.ds(tt, tb, stride=seq), :] for tt in range(seq)]
        for e in range(buf):
            pool_out_ref[e, :, ln] = ext[seq + e]
        for tt in range(seq):
            acc = ext[buf + tt]
            for j in range(1, w):
                acc = acc + ext[buf + tt - j]
            e_scr[g, pl.ds(tt, tb, stride=seq), :] = acc * (1.0 / w) - ext[buf + tt]
        pooled.append(e_scr[g])
    col_ga = pw + 4 * hw
    ya = _pool_project(pooled, w_pool_mix_ref, pool_scale_ref, w_pool_up_ref)

    z_scr[...] = _dot(h, w_in_ref[:, pw:pw + 4 * hw])
    lb = _forget_lower_bound(lb_ref)
    pos8 = lax.broadcasted_iota(jnp.int32, (1, SUBLANES, LANES), 1)
    for hh in range(n_heads):
        ln = slice(hh * LANES, (hh + 1) * LANES)
        zq = z_scr[:, hh * LANES:(hh + 1) * LANES]
        zf = z_scr[:, hw + hh * LANES:hw + (hh + 1) * LANES]
        lbh = lb[:, ln]
        fg = lbh + (1.0 - lbh) * jax.nn.sigmoid(zf)
        k = 1.0 - fg
        q = _silu(zq)
        y = _scan_rows8(jnp.log(fg), pos8)
        tot = _last_row(y)
        cs = y.reshape(rows_n, LANES)
        qe_scr[:, ln] = q * jnp.exp(cs)
        ke_scr[:, ln] = k * jnp.exp(-cs)
        kd_scr[:, ln] = k * jnp.exp((tot - y).reshape(rows_n, LANES))
        dec_scr[:, ln] = jnp.exp(tot).reshape(rows_n, LANES)

    ti = lax.broadcasted_iota(jnp.int32, (seq, seq), 0)
    si = lax.broadcasted_iota(jnp.int32, (seq, seq), 1)
    causal = si <= ti

    def seq_body(b, carry):
        r0 = pl.multiple_of(b * seq, seq)
        rows = pl.ds(r0, seq)
        for hh in range(n_heads):
            ln = slice(hh * LANES, (hh + 1) * LANES)
            qe = qe_scr[rows, ln]
            v = z_scr[rows, 2 * hw + hh * LANES:2 * hw + (hh + 1) * LANES]
            a = lax.dot_general(_bf(qe), _bf(ke_scr[rows, ln]), _NT, preferred_element_type=F32)
            o, s_new = _finish_head(jnp.where(causal, a, 0.0), v, sh_ref[b, hh], qe,
                                    kd_scr[rows, ln], dec_scr[pl.ds(r0, 1), ln])
            hgrn_out_ref[b, hh] = s_new
            o_scr[rows, ln] = o
        return carry

    lax.fori_loop(0, tb, seq_body, 0, unroll=4)

    gates = _dot(h, w_in_ref[:, col_ga:col_ga + 2 * d_model])
    x1_ref[...] = _gated_merge_out(x, ya, gates[:, :d_model], gates[:, d_model:], o_scr[...],
                                   z_scr[:, 3 * hw:4 * hw], hgrn_norm_ref, w_hgrn_up_ref, w_out_ref,
                                   n_heads)


def _ffn_ple_kernel(xp_ref, xs_ref, pp_ref, ps_ref, g_ffn_ref, w_gate_ref, w_up_ref, w_down_ref,
                    g_ple_ref, w_ple_gate_ref, w_ple_proj_ref, g_final_ref, yp_ref, ys_ref, *, n_prompt):
    d_ff = w_gate_ref.shape[1]
    ff_cuts = list(range(0, d_ff, FFN_PIECE)) + [d_ff]

    def row_group(x_ref, p_ref, y_ref, rows):
        st = {"act": []}

        def norm():
            st["x"] = x_ref[rows, :]
            st["h2"] = _bf(_rms(st["x"], g_ffn_ref[...]))

        def ff(c0, c1):
            def run():
                g = _dot(st["h2"], w_gate_ref[:, c0:c1])
                st["act"].append(_bf(_silu(g) * _dot(st["h2"], w_up_ref[:, c0:c1])))
            return run

        def down():
            x = st["x"] + _dot(jnp.concatenate(st["act"], axis=1), w_down_ref[...])
            st["x"] = x
            st["h3"] = _bf(_rms(x, g_ple_ref[...]))

        def ple():
            gate = jax.nn.sigmoid(_dot(st["h3"], w_ple_gate_ref[...]))
            emb = _dot(_bf(p_ref[rows, :]), w_ple_proj_ref[...])
            y_ref[rows, :] = _rms(st["x"] + gate * emb, g_final_ref[...])

        return [[[norm]], [[ff(c0, c1) for c0, c1 in zip(ff_cuts, ff_cuts[1:])]], [[down]], [[ple]]]

    def tile(x_ref, p_ref, y_ref):
        n = x_ref.shape[0] // FFN_ROW_GROUPS
        _emit_staggered([row_group(x_ref, p_ref, y_ref, slice(r * n, (r + 1) * n))
                         for r in range(FFN_ROW_GROUPS)])

    i = pl.program_id(0)

    @pl.when(i < n_prompt)
    def _():
        tile(xp_ref, pp_ref, yp_ref)

    @pl.when(i >= n_prompt)
    def _():
        tile(xs_ref, ps_ref, ys_ref)


def _whole(_):
    return pl.BlockSpec(memory_space=pltpu.VMEM)


def _mixer_prompt(x, weights, to_cast, *, tile, chunk):
    bsz, seq, d_model = x.shape
    pw = weights[3].shape[1]
    hw = weights[5].shape[1]
    n_heads = hw // LANES
    ns = PROMPT_STREAMS
    assert seq % tile == 0 and tile % chunk == 0 and chunk % SUB == 0 and bsz % ns == 0
    nt = seq // tile
    n_steps = (bsz // ns) * nt
    slabs = []
    for w in to_cast:
        rep = 1 if w.shape[0] % (n_steps * 2 * SUBLANES) == 0 else 2
        rows = w.shape[0] * rep // n_steps
        assert rows * n_steps == w.shape[0] * rep and rows % (2 * SUBLANES) == 0
        slabs.append(pl.BlockSpec((rows, w.shape[1]), lambda b, t, rep=rep: ((b * nt + t) // rep, 0)))
    outs = pl.pallas_call(
        functools.partial(_mixer_prompt_kernel, chunk=chunk, n_cast=len(to_cast)),
        grid=(bsz // ns, nt),
        in_specs=([pl.BlockSpec((ns, tile, d_model), lambda b, t: (b, t, 0))]
                  + [_whole(w) for w in weights] + slabs),
        out_specs=[
            pl.BlockSpec((ns, tile, d_model), lambda b, t: (b, t, 0)),
            pl.BlockSpec((ns, POOL_HALO, pw), lambda b, t: (b, 0, 0)),
            pl.BlockSpec((ns, n_heads, LANES, LANES), lambda b, t: (b, 0, 0, 0)),
        ] + slabs,
        out_shape=[
            jax.ShapeDtypeStruct((bsz, seq, d_model), F32),
            jax.ShapeDtypeStruct((bsz, POOL_HALO, pw), F32),
            jax.ShapeDtypeStruct((bsz, n_heads, LANES, LANES), F32),
        ] + [jax.ShapeDtypeStruct(w.shape, BF16) for w in to_cast],
        scratch_shapes=[
            pltpu.VMEM((ns, 3, POOL_TOP + tile, pw), F32),
            pltpu.VMEM((ns, tile, 4 * hw), F32),
            pltpu.VMEM((ns, tile, hw), F32),
            pltpu.VMEM((ns, n_heads, LANES, LANES), F32),
        ],
        compiler_params=pltpu.CompilerParams(
            dimension_semantics=("parallel", "arbitrary"), vmem_limit_bytes=VMEM_LIMIT),
        name="mixer_prompt",
    )(x, *weights, *to_cast)
    return outs[0], outs[1], outs[2], outs[3:]


def _mixer_sample(x, state_pool, state_hgrn, weights, *, tile_b):
    bsz, seq, d_model = x.shape
    buf, _, pw = state_pool.shape
    _, n_heads, dk, dv = state_hgrn.shape
    hw = n_heads * dv
    assert bsz % tile_b == 0 and seq == SUBLANES and buf >= max(POOL_WINDOWS) - 1
    assert dk == LANES and dv == LANES
    rows = tile_b * seq
    xf = x.reshape(bsz * seq, d_model)
    return pl.pallas_call(
        functools.partial(_mixer_sample_kernel, seq=seq),
        grid=(bsz // tile_b,),
        in_specs=[
            pl.BlockSpec((rows, d_model), lambda i: (i, 0)),
            pl.BlockSpec((buf, tile_b, pw), lambda i: (0, i, 0)),
            pl.BlockSpec((tile_b, n_heads, dk, dv), lambda i: (i, 0, 0, 0)),
        ] + [_whole(w) for w in weights],
        out_specs=[
            pl.BlockSpec((rows, d_model), lambda i: (i, 0)),
            pl.BlockSpec((buf, tile_b, pw), lambda i: (0, i, 0)),
            pl.BlockSpec((tile_b, n_heads, dk, dv), lambda i: (i, 0, 0, 0)),
        ],
        out_shape=[
            jax.ShapeDtypeStruct((bsz * seq, d_model), F32),
            jax.ShapeDtypeStruct((buf, bsz, pw), F32),
            jax.ShapeDtypeStruct((bsz, n_heads, dk, dv), F32),
        ],
        scratch_shapes=[
            pltpu.VMEM((pw // LANES, rows, LANES), F32),
            pltpu.VMEM((rows, 4 * hw), F32),
            pltpu.VMEM((rows, hw), F32),
            pltpu.VMEM((rows, hw), F32),
            pltpu.VMEM((rows, hw), F32),
            pltpu.VMEM((rows, hw), F32),
            pltpu.VMEM((rows, hw), F32),
        ],
        compiler_params=pltpu.CompilerParams(
            dimension_semantics=("parallel",), vmem_limit_bytes=VMEM_LIMIT),
        name="mixer_sample",
    )(xf, state_pool, state_hgrn, *weights)


def _ffn_ple(x_p, x_s, p_p, p_s, weights, *, tile):
    (n_p, d_model), n_s = x_p.shape, x_s.shape[0]
    p_dim = p_p.shape[1]
    assert n_p % tile == 0 and n_s % tile == 0
    tp, ts = n_p // tile, n_s // tile

    def prompt_idx(i):
        return (jnp.minimum(i, tp - 1), 0)

    def sample_idx(i):
        return (jnp.maximum(i - tp, 0), 0)

    return pl.pallas_call(
        functools.partial(_ffn_ple_kernel, n_prompt=tp),
        grid=(tp + ts,),
        in_specs=[
            pl.BlockSpec((tile, d_model), prompt_idx),
            pl.BlockSpec((tile, d_model), sample_idx),
            pl.BlockSpec((tile, p_dim), prompt_idx),
            pl.BlockSpec((tile, p_dim), sample_idx),
        ] + [_whole(w) for w in weights],
        out_specs=[
            pl.BlockSpec((tile, d_model), prompt_idx),
            pl.BlockSpec((tile, d_model), sample_idx),
        ],
        out_shape=[
            jax.ShapeDtypeStruct((n_p, d_model), F32),
            jax.ShapeDtypeStruct((n_s, d_model), F32),
        ],
        compiler_params=pltpu.CompilerParams(
            dimension_semantics=("arbitrary",), vmem_limit_bytes=VMEM_LIMIT),
        name="ffn_ple",
    )(x_p, x_s, p_p, p_s, *weights)


def kernel(x_prompt, x_sample, state_pool, state_hgrn, p_prompt, p_sample, g_mix, w_in, w_pool_mix, pool_scale, hgrn_lb, hgrn_norm, w_pool_up, w_hgrn_up, w_out, g_ffn, w_ffn_gate, w_ffn_up, w_ffn_down, g_ple, w_ple_gate, w_ple_proj, g_final):
    depth = w_in.shape[0]
    assert depth == 1 and hgrn_lb.shape[0] == 2
    bsz, seq, d_model = x_prompt.shape
    dbsz, dseq, _ = x_sample.shape
    buf = state_pool.shape[2]

    mixer_w = (g_mix, _bf(w_in[0]), _bf(w_pool_mix[0]), pool_scale, hgrn_lb, hgrn_norm,
               _bf(w_pool_up[0]), _bf(w_hgrn_up[0]), _bf(w_out[0]))
    x1_p, pool_p, hgrn_p, (wg, wu, wd, wpg, wpp) = _mixer_prompt(
        x_prompt, mixer_w, (w_ffn_gate[0], w_ffn_up[0], w_ffn_down[0], w_ple_gate[0], w_ple_proj[0]),
        tile=512, chunk=128)
    ffn_w = (g_ffn, wg, wu, wd, g_ple, wpg, wpp, g_final.reshape(1, d_model))
    x1_s, pool_s, hgrn_s = _mixer_sample(x_sample, jnp.swapaxes(state_pool[0], 0, 1), state_hgrn[0],
                                         mixer_w, tile_b=16)

    y_p, y_s = _ffn_ple(x1_p.reshape(bsz * seq, d_model), x1_s,
                        p_prompt[0].reshape(bsz * seq, -1), p_sample[0].reshape(dbsz * dseq, -1),
                        ffn_w, tile=512)

    return (y_p.reshape(bsz, seq, d_model), y_s.reshape(dbsz, dseq, d_model),
            pool_p[None, :, POOL_HALO - buf:, :], hgrn_p[None],
            jnp.swapaxes(pool_s, 0, 1)[None], hgrn_s[None])
```

```python
import functools

import jax
import jax.numpy as jnp
from jax import lax
from jax.experimental import pallas as pl
from jax.experimental.pallas import tpu as pltpu

F32 = jnp.float32
BF16 = jnp.bfloat16

EPS = 1e-6
POOL_WINDOWS = (2, 4, 8, 16)
assert POOL_WINDOWS == tuple(2 << g for g in range(len(POOL_WINDOWS)))
POOL_HALO = 16
POOL_PAD = 8
POOL_TOP = POOL_PAD + POOL_HALO
SUB = 16
LANES = 128
SUBLANES = 8
VMEM_LIMIT = 56 * 1024 * 1024
GATE_PIECE = 256
FFN_ROW_GROUPS = 4
FFN_PIECE = 768
SAMPLE_UNROLL = 4
PROMPT_STREAMS = 1

_NT = (((1,), (1,)), ((), ()))
_TN = (((0,), (0,)), ((), ()))


def _bf(x):
    return x.astype(BF16)


def _dot(a, b):
    return jnp.dot(a, b, preferred_element_type=F32)


def _rms(x, gain):
    ms = jnp.mean(x * x, axis=-1, keepdims=True)
    return x * lax.rsqrt(ms + EPS) * gain


def _silu(x):
    return x * jax.nn.sigmoid(x)


def _forget_lower_bound(lb_ref):
    a0 = lb_ref[0:1, :]
    a1 = lb_ref[1:2, :]
    m = jnp.maximum(a0, a1)
    e0 = jnp.exp(a0 - m)
    e1 = jnp.exp(a1 - m)
    return e0 / (e0 + e1)


def _scan_rows8(x, pos8):
    y = x.reshape(x.shape[0] // SUBLANES, SUBLANES, x.shape[1])
    s = 1
    while s < SUBLANES:
        y = y + jnp.where(pos8 >= s, pltpu.roll(y, s, axis=1), 0.0)
        s *= 2
    return y


def _last_row(y):
    return jnp.broadcast_to(y[..., SUBLANES - 1:SUBLANES, :], y.shape)


def _finish_head(a, v, s_prev, q_big, k_big, dec_row):
    vb = _bf(v)
    if a.shape[1] % LANES == 0:
        o = _dot(jnp.concatenate([_bf(a), _bf(q_big)], axis=1),
                 jnp.concatenate([vb, _bf(s_prev)], axis=0))
    else:
        o = _dot(_bf(a), vb) + _dot(_bf(q_big), _bf(s_prev))
    ds = lax.dot_general(_bf(k_big), vb, _TN, preferred_element_type=F32)
    dk = s_prev.shape[0]
    dec = jnp.transpose(jnp.broadcast_to(dec_row, (dk, dk)))
    return o, dec * s_prev + ds


def _scale_blocks(x, factors):
    out = []
    for i, f in enumerate(factors):
        blk = x[i * SUB:(i + 1) * SUB]
        out.append(blk if f is None else blk * jnp.concatenate([f] * (SUB // SUBLANES), axis=0))
    return jnp.concatenate(out, axis=0)


def _chunk_front(q, k, logf, pos8):
    n = q.shape[0]
    nb = n // SUB
    y = _scan_rows8(logf, pos8)
    cs_blk, tots = [], []
    for i in range(nb):
        lo = y[2 * i]
        hi = y[2 * i + 1] + _last_row(y[2 * i])
        cs_blk += [lo, hi]
        tots.append(_last_row(hi))
    cs = jnp.concatenate(cs_blk, axis=0)
    rs = jnp.concatenate([t for t in tots for _ in range(SUB // SUBLANES)], axis=0) - cs
    ecs = jnp.exp(cs)
    qe = q * ecs
    ke = k * jnp.exp(-cs)
    kd = k * jnp.exp(rs)

    def span(lo, hi):
        acc = None
        for m in range(lo, hi):
            acc = tots[m] if acc is None else acc + tots[m]
        return acc

    def expo(t):
        return None if t is None else jnp.exp(t)

    levels = []
    w = 2
    while w * SUB < n:
        qf = [expo(span((i // w) * w, i)) if (i // w) % 2 == 1 else None for i in range(nb)]
        kf = [expo(span(i + 1, (i // w + 1) * w)) if (i // w) % 2 == 0 else None for i in range(nb)]
        levels.append((_bf(_scale_blocks(qe, qf)), _bf(_scale_blocks(kd, kf))))
        w *= 2
    return {
        "q01": _bf(qe), "k01": _bf(jnp.concatenate([ke, kd], axis=0)),
        "levels": levels,
        "q_big": _bf(_scale_blocks(qe, [expo(span(0, i)) for i in range(nb)])),
        "k_big": _bf(_scale_blocks(kd, [expo(span(i + 1, nb)) for i in range(nb)])),
        "dec_row": jnp.exp(span(0, nb))[0:1, :],
    }


def _chunk_back(fr, v, s_prev, masks):
    n = fr["q01"].shape[0]
    a01 = lax.dot_general(fr["q01"], fr["k01"], _NT, preferred_element_type=F32)
    a = jnp.where(masks[0], a01[:, :n], jnp.where(masks[1], a01[:, n:], 0.0))
    for lvl, (q_l, k_l) in enumerate(fr["levels"]):
        a_l = lax.dot_general(q_l, k_l, _NT, preferred_element_type=F32)
        a = jnp.where(masks[lvl + 2], a_l, a)
    return _finish_head(a, v, s_prev, fr["q_big"], fr["k_big"], fr["dec_row"])


def _gated_merge_out(x, ya, ga, gb, o_raw, zg, hgrn_norm_ref, w_hgrn_up_ref, w_out_ref, n_heads):
    pieces = []
    for hh in range(n_heads):
        ln = slice(hh * LANES, (hh + 1) * LANES)
        oh = o_raw[:, ln]
        oh = oh * lax.rsqrt(jnp.mean(oh * oh, axis=-1, keepdims=True) + EPS)
        pieces.append(oh * hgrn_norm_ref[:, ln])
    o = jnp.concatenate(pieces, axis=-1) * _silu(zg)
    yb = _dot(_bf(o), w_hgrn_up_ref[...])
    merged = jax.nn.sigmoid(ga) * ya + jax.nn.sigmoid(gb) * yb
    return x + _dot(_bf(merged), w_out_ref[...])


def _pool_project(pooled, w_pool_mix_ref, pool_scale_ref, w_pool_up_ref):
    pieces = [_dot(_bf(pg), w_pool_mix_ref[g]) for g, pg in enumerate(pooled)]
    pool_out = jnp.concatenate(pieces, axis=-1) * pool_scale_ref[...]
    return _dot(_bf(pool_out), w_pool_up_ref[...])


def _prompt_stream(s, t, x_ref, g_mix_ref, w_in_ref, w_pool_mix_ref, pool_scale_ref, lb_ref,
                   hgrn_norm_ref, w_pool_up_ref, w_hgrn_up_ref, w_out_ref, x1_ref,
                   u_scr, z_scr, o_scr, s_scr, chunk, pos8, masks):
    _, tm, d_model = x_ref.shape
    pw = u_scr.shape[3]
    hw = o_scr.shape[2]
    n_heads = hw // LANES
    col_ga = pw + 4 * hw
    st = {"gates": [], "pooled": [None] * len(POOL_WINDOWS)}

    def norm():
        st["h"] = _bf(_rms(x_ref[s], g_mix_ref[...]))
        st["lb"] = _forget_lower_bound(lb_ref)

    def proj_u():
        u_scr[s, 0, POOL_TOP:POOL_TOP + tm, :] = _dot(st["h"], w_in_ref[:, 0:pw])

    def proj_z(c0, c1):
        def run():
            z_scr[s, :, c0:c1] = _dot(st["h"], w_in_ref[:, pw + c0:pw + c1])
        return run

    def window_level(lv):
        def run():
            shift = 1 << lv
            lanes = slice(lv * LANES, pw)
            src = 0 if lv == 0 else 1 + (lv - 1) % 2
            rows_all = POOL_HALO + tm
            sums = (u_scr[s, src, POOL_PAD:POOL_PAD + rows_all, lanes]
                    + u_scr[s, src, pl.ds(POOL_PAD - shift, rows_all), lanes])
            if lanes.start + LANES < pw:
                u_scr[s, 1 + lv % 2, POOL_PAD:POOL_PAD + rows_all, lanes] = sums
            seen = lax.broadcasted_iota(jnp.int32, (tm, 1), 0) + (t * tm + 1)
            inv_cnt = 1.0 / jnp.minimum(seen, 2 * shift).astype(F32)
            ug = u_scr[s, 0, POOL_TOP:POOL_TOP + tm, lv * LANES:(lv + 1) * LANES]
            st["pooled"][lv] = sums[POOL_HALO:, 0:LANES] * inv_cnt - ug
        return run

    def pool():
        st["ya"] = _pool_project(st["pooled"], w_pool_mix_ref, pool_scale_ref, w_pool_up_ref)
        u_scr[s, 0, POOL_PAD:POOL_TOP, :] = u_scr[s, 0, POOL_PAD + tm:POOL_TOP + tm, :]

    steps = [(c, hh) for c in range(tm // chunk) for hh in range(n_heads)]
    half = n_heads // 2 * LANES

    def front(k):
        c, hh = steps[k]

        def run():
            rows = slice(c * chunk, (c + 1) * chunk)
            zq = z_scr[s, rows, hh * LANES:(hh + 1) * LANES]
            zf = z_scr[s, rows, hw + hh * LANES:hw + (hh + 1) * LANES]
            lbh = st["lb"][:, hh * LANES:(hh + 1) * LANES]
            fg = lbh + (1.0 - lbh) * jax.nn.sigmoid(zf)
            st["front"][k] = _chunk_front(_silu(zq), 1.0 - fg, jnp.log(fg), pos8)
        return run

    def back(k):
        c, hh = steps[k]

        def run():
            rows = slice(c * chunk, (c + 1) * chunk)
            v = z_scr[s, rows, 2 * hw + hh * LANES:2 * hw + (hh + 1) * LANES]
            o, s_new = _chunk_back(st["front"][k], v, s_scr[s, hh], masks)
            s_scr[s, hh] = s_new
            o_scr[s, rows, hh * LANES:(hh + 1) * LANES] = o
        return run

    def gate(c0):
        def run():
            st["gates"].append(_dot(st["h"], w_in_ref[:, c0:c0 + GATE_PIECE]))
        return run

    def tail():
        gates = jnp.concatenate(st["gates"], axis=1)
        x1_ref[s] = _gated_merge_out(x_ref[s], st["ya"], gates[:, :d_model], gates[:, d_model:],
                                     o_scr[s], z_scr[s, :, 3 * hw:4 * hw], hgrn_norm_ref,
                                     w_hgrn_up_ref, w_out_ref, n_heads)

    st["front"] = [None] * len(steps)
    gate_cols = [(col_ga + j * GATE_PIECE) for j in range(2 * d_model // GATE_PIECE)]
    early = [k for k, (_, hh) in enumerate(steps) if hh < n_heads // 2]
    late = [k for k, (_, hh) in enumerate(steps) if hh >= n_heads // 2]
    return [
        [[norm]],
        [[proj_u, proj_z(0, half), proj_z(hw, hw + half)]],
        [[proj_z(half, hw), proj_z(hw + half, 2 * hw), proj_z(2 * hw, 3 * hw), proj_z(3 * hw, 4 * hw)],
         [front(k) for k in early + late], [window_level(lv) for lv in range(len(POOL_WINDOWS))]],
        [[back(k) for k in range(len(steps))], [gate(c0) for c0 in gate_cols] + [pool]],
        [[tail]],
    ]


def _merge_evenly(lists):
    keyed = [((i + 0.5) / len(lst), n, fn) for n, lst in enumerate(lists) for i, fn in enumerate(lst)]
    return [fn for _, _, fn in sorted(keyed, key=lambda e: e[:2])]


def _emit_staggered(streams):
    n_stages = len(streams[0])
    for slot in range(n_stages + len(streams) - 1):
        lists = []
        for s, stages in enumerate(streams):
            if 0 <= slot - s < n_stages:
                lists += stages[slot - s]
        for phase in _merge_evenly(lists):
            phase()


def _mixer_prompt_kernel(x_ref, g_mix_ref, w_in_ref, w_pool_mix_ref, pool_scale_ref, lb_ref,
                         hgrn_norm_ref, w_pool_up_ref, w_hgrn_up_ref, w_out_ref, *rest, chunk, n_cast):
    cast_in = rest[:n_cast]
    x1_ref, pool_out_ref, hgrn_out_ref = rest[n_cast:n_cast + 3]
    cast_out = rest[n_cast + 3:2 * n_cast + 3]
    u_scr, z_scr, o_scr, s_scr = rest[2 * n_cast + 3:]
    n_seq = x_ref.shape[0]
    t = pl.program_id(1)

    @pl.when(t == 0)
    def _():
        u_scr[:, :, 0:POOL_TOP, :] = jnp.zeros(u_scr.shape[:2] + (POOL_TOP, u_scr.shape[3]), F32)
        s_scr[...] = jnp.zeros(s_scr.shape, F32)

    pos8 = lax.broadcasted_iota(jnp.int32, (1, SUBLANES, LANES), 1)
    ti = lax.broadcasted_iota(jnp.int32, (chunk, chunk), 0)
    si = lax.broadcasted_iota(jnp.int32, (chunk, chunk), 1)
    masks = [((ti // SUB) == (si // SUB)) & (si <= ti)]
    b = SUB
    while b < chunk:
        masks.append(((ti // (2 * b)) == (si // (2 * b))) & ((ti & b) != 0) & ((si & b) == 0))
        b *= 2

    streams = [_prompt_stream(s, t, x_ref, g_mix_ref, w_in_ref, w_pool_mix_ref, pool_scale_ref, lb_ref,
                              hgrn_norm_ref, w_pool_up_ref, w_hgrn_up_ref, w_out_ref, x1_ref,
                              u_scr, z_scr, o_scr, s_scr, chunk, pos8, masks) for s in range(n_seq)]

    def cast_slabs():
        for src, dst in zip(cast_in, cast_out):
            dst[...] = _bf(src[...])

    streams[0][1].append([cast_slabs])
    _emit_staggered(streams)

    @pl.when(t == pl.num_programs(1) - 1)
    def _():
        pool_out_ref[...] = u_scr[:, 0, POOL_PAD:POOL_TOP, :]
        hgrn_out_ref[...] = s_scr[...]


def _mixer_sample_kernel(x_ref, sp_ref, sh_ref, g_mix_ref, w_in_ref, w_pool_mix_ref, pool_scale_ref,
                         lb_ref, hgrn_norm_ref, w_pool_up_ref, w_hgrn_up_ref, w_out_ref,
                         x1_ref, pool_out_ref, hgrn_out_ref,
                         e_scr, z_scr, o_scr, qe_scr, ke_scr, kd_scr, dec_scr, *, seq):
    rows_n, d_model = x_ref.shape
    buf, tb, pw = sp_ref.shape
    hw = o_scr.shape[1]
    n_heads = hw // LANES

    x = x_ref[...]
    h = _bf(_rms(x, g_mix_ref[...]))

    u = _dot(h, w_in_ref[:, 0:pw])
    pooled = []
    for g, w in enumerate(POOL_WINDOWS):
        ln = slice(g * LANES, (g + 1) * LANES)
        e_scr[g] = u[:, ln]
        ext = [sp_ref[e, :, ln] for e in range(buf)]
        ext += [e_scr[g, pl.ds(tt, tb, stride=seq), :] for tt in range(seq)]
        for e in range(buf):
            pool_out_ref[e, :, ln] = ext[seq + e]
        for tt in range(seq):
            acc = ext[buf + tt]
            for j in range(1, w):
                acc = acc + ext[buf + tt - j]
            e_scr[g, pl.ds(tt, tb, stride=seq), :] = acc * (1.0 / w) - ext[buf + tt]
        pooled.append(e_scr[g])
    col_ga = pw + 4 * hw
    ya = _pool_project(pooled, w_pool_mix_ref, pool_scale_ref, w_pool_up_ref)

    z_scr[...] = _dot(h, w_in_ref[:, pw:pw + 4 * hw])
    lb = _forget_lower_bound(lb_ref)
    pos8 = lax.broadcasted_iota(jnp.int32, (1, SUBLANES, LANES), 1)
    for hh in range(n_heads):
        ln = slice(hh * LANES, (hh + 1) * LANES)
        zq = z_scr[:, hh * LANES:(hh + 1) * LANES]
        zf = z_scr[:, hw + hh * LANES:hw + (hh + 1) * LANES]
        lbh = lb[:, ln]
        fg = lbh + (1.0 - lbh) * jax.nn.sigmoid(zf)
        k = 1.0 - fg
        q = _silu(zq)
        y = _scan_rows8(jnp.log(fg), pos8)
        tot = _last_row(y)
        cs = y.reshape(rows_n, LANES)
        qe_scr[:, ln] = q * jnp.exp(cs)
        ke_scr[:, ln] = k * jnp.exp(-cs)
        kd_scr[:, ln] = k * jnp.exp((tot - y).reshape(rows_n, LANES))
        dec_scr[:, ln] = jnp.exp(tot).reshape(rows_n, LANES)

    ti = lax.broadcasted_iota(jnp.int32, (seq, seq), 0)
    si = lax.broadcasted_iota(jnp.int32, (seq, seq), 1)
    causal = si <= ti

    def seq_body(b, carry):
        r0 = pl.multiple_of(b * seq, seq)
        rows = pl.ds(r0, seq)
        for hh in range(n_heads):
            ln = slice(hh * LANES, (hh + 1) * LANES)
            qe = qe_scr[rows, ln]
            v = z_scr[rows, 2 * hw + hh * LANES:2 * hw + (hh + 1) * LANES]
            a = lax.dot_general(_bf(qe), _bf(ke_scr[rows, ln]), _NT, preferred_element_type=F32)
            o, s_new = _finish_head(jnp.where(causal, a, 0.0), v, sh_ref[b, hh], qe,
                                    kd_scr[rows, ln], dec_scr[pl.ds(r0, 1), ln])
            hgrn_out_ref[b, hh] = s_new
            o_scr[rows, ln] = o
        return carry

    lax.fori_loop(0, tb, seq_body, 0, unroll=SAMPLE_UNROLL)

    gates = _dot(h, w_in_ref[:, col_ga:col_ga + 2 * d_model])
    x1_ref[...] = _gated_merge_out(x, ya, gates[:, :d_model], gates[:, d_model:], o_scr[...],
                                   z_scr[:, 3 * hw:4 * hw], hgrn_norm_ref, w_hgrn_up_ref, w_out_ref,
                                   n_heads)


def _ffn_ple_kernel(xp_ref, xs_ref, pp_ref, ps_ref, g_ffn_ref, w_gate_ref, w_up_ref, w_down_ref,
                    g_ple_ref, w_ple_gate_ref, w_ple_proj_ref, g_final_ref, yp_ref, ys_ref, *, n_prompt):
    d_ff = w_gate_ref.shape[1]
    ff_cuts = list(range(0, d_ff, FFN_PIECE)) + [d_ff]

    def row_group(x_ref, p_ref, y_ref, rows):
        st = {"act": []}

        def norm():
            st["x"] = x_ref[rows, :]
            st["h2"] = _bf(_rms(st["x"], g_ffn_ref[...]))

        def ff(c0, c1):
            def run():
                g = _dot(st["h2"], w_gate_ref[:, c0:c1])
                st["act"].append(_bf(_silu(g) * _dot(st["h2"], w_up_ref[:, c0:c1])))
            return run

        def down():
            x = st["x"] + _dot(jnp.concatenate(st["act"], axis=1), w_down_ref[...])
            st["x"] = x
            st["h3"] = _bf(_rms(x, g_ple_ref[...]))

        def ple():
            gate = jax.nn.sigmoid(_dot(st["h3"], w_ple_gate_ref[...]))
            emb = _dot(_bf(p_ref[rows, :]), w_ple_proj_ref[...])
            y_ref[rows, :] = _rms(st["x"] + gate * emb, g_final_ref[...])

        return [[[norm]], [[ff(c0, c1) for c0, c1 in zip(ff_cuts, ff_cuts[1:])]], [[down]], [[ple]]]

    def tile(x_ref, p_ref, y_ref):
        n = x_ref.shape[0] // FFN_ROW_GROUPS
        _emit_staggered([row_group(x_ref, p_ref, y_ref, slice(r * n, (r + 1) * n))
                         for r in range(FFN_ROW_GROUPS)])

    i = pl.program_id(0)

    @pl.when(i < n_prompt)
    def _():
        tile(xp_ref, pp_ref, yp_ref)

    @pl.when(i >= n_prompt)
    def _():
        tile(xs_ref, ps_ref, ys_ref)


def _whole(_):
    return pl.BlockSpec(memory_space=pltpu.VMEM)


def _mixer_prompt(x, weights, to_cast, *, tile, chunk):
    bsz, seq, d_model = x.shape
    pw = weights[3].shape[1]
    hw = weights[5].shape[1]
    n_heads = hw // LANES
    ns = PROMPT_STREAMS
    assert seq % tile == 0 and tile % chunk == 0 and chunk % SUB == 0 and bsz % ns == 0
    nt = seq // tile
    n_steps = (bsz // ns) * nt
    slabs = []
    for w in to_cast:
        rep = 1 if w.shape[0] % (n_steps * 2 * SUBLANES) == 0 else 2
        rows = w.shape[0] * rep // n_steps
        assert rows * n_steps == w.shape[0] * rep and rows % (2 * SUBLANES) == 0
        slabs.append(pl.BlockSpec((rows, w.shape[1]), lambda b, t, rep=rep: ((b * nt + t) // rep, 0)))
    outs = pl.pallas_call(
        functools.partial(_mixer_prompt_kernel, chunk=chunk, n_cast=len(to_cast)),
        grid=(bsz // ns, nt),
        in_specs=([pl.BlockSpec((ns, tile, d_model), lambda b, t: (b, t, 0))]
                  + [_whole(w) for w in weights] + slabs),
        out_specs=[
            pl.BlockSpec((ns, tile, d_model), lambda b, t: (b, t, 0)),
            pl.BlockSpec((ns, POOL_HALO, pw), lambda b, t: (b, 0, 0)),
            pl.BlockSpec((ns, n_heads, LANES, LANES), lambda b, t: (b, 0, 0, 0)),
        ] + slabs,
        out_shape=[
            jax.ShapeDtypeStruct((bsz, seq, d_model), F32),
            jax.ShapeDtypeStruct((bsz, POOL_HALO, pw), F32),
            jax.ShapeDtypeStruct((bsz, n_heads, LANES, LANES), F32),
        ] + [jax.ShapeDtypeStruct(w.shape, BF16) for w in to_cast],
        scratch_shapes=[
            pltpu.VMEM((ns, 3, POOL_TOP + tile, pw), F32),
            pltpu.VMEM((ns, tile, 4 * hw), F32),
            pltpu.VMEM((ns, tile, hw), F32),
            pltpu.VMEM((ns, n_heads, LANES, LANES), F32),
        ],
        compiler_params=pltpu.CompilerParams(
            dimension_semantics=("parallel", "arbitrary"), vmem_limit_bytes=VMEM_LIMIT),
        name="mixer_prompt",
    )(x, *weights, *to_cast)
    return outs[0], outs[1], outs[2], outs[3:]


def _mixer_sample(x, state_pool, state_hgrn, weights, *, tile_b):
    bsz, seq, d_model = x.shape
    buf, _, pw = state_pool.shape
    _, n_heads, dk, dv = state_hgrn.shape
    hw = n_heads * dv
    assert bsz % tile_b == 0 and seq == SUBLANES and buf >= max(POOL_WINDOWS) - 1
    assert dk == LANES and dv == LANES
    rows = tile_b * seq
    xf = x.reshape(bsz * seq, d_model)
    return pl.pallas_call(
        functools.partial(_mixer_sample_kernel, seq=seq),
        grid=(bsz // tile_b,),
        in_specs=[
            pl.BlockSpec((rows, d_model), lambda i: (i, 0)),
            pl.BlockSpec((buf, tile_b, pw), lambda i: (0, i, 0)),
            pl.BlockSpec((tile_b, n_heads, dk, dv), lambda i: (i, 0, 0, 0)),
        ] + [_whole(w) for w in weights],
        out_specs=[
            pl.BlockSpec((rows, d_model), lambda i: (i, 0)),
            pl.BlockSpec((buf, tile_b, pw), lambda i: (0, i, 0)),
            pl.BlockSpec((tile_b, n_heads, dk, dv), lambda i: (i, 0, 0, 0)),
        ],
        out_shape=[
            jax.ShapeDtypeStruct((bsz * seq, d_model), F32),
            jax.ShapeDtypeStruct((buf, bsz, pw), F32),
            jax.ShapeDtypeStruct((bsz, n_heads, dk, dv), F32),
        ],
        scratch_shapes=[
            pltpu.VMEM((pw // LANES, rows, LANES), F32),
            pltpu.VMEM((rows, 4 * hw), F32),
            pltpu.VMEM((rows, hw), F32),
            pltpu.VMEM((rows, hw), F32),
            pltpu.VMEM((rows, hw), F32),
            pltpu.VMEM((rows, hw), F32),
            pltpu.VMEM((rows, hw), F32),
        ],
        compiler_params=pltpu.CompilerParams(
            dimension_semantics=("parallel",), vmem_limit_bytes=VMEM_LIMIT),
        name="mixer_sample",
    )(xf, state_pool, state_hgrn, *weights)


def _ffn_ple(x_p, x_s, p_p, p_s, weights, *, tile):
    (n_p, d_model), n_s = x_p.shape, x_s.shape[0]
    p_dim = p_p.shape[1]
    assert n_p % tile == 0 and n_s % tile == 0
    tp, ts = n_p // tile, n_s // tile

    def prompt_idx(i):
        return (jnp.minimum(i, tp - 1), 0)

    def sample_idx(i):
        return (jnp.maximum(i - tp, 0), 0)

    return pl.pallas_call(
        functools.partial(_ffn_ple_kernel, n_prompt=tp),
        grid=(tp + ts,),
        in_specs=[
            pl.BlockSpec((tile, d_model), prompt_idx),
            pl.BlockSpec((tile, d_model), sample_idx, pipeline_mode=pl.Buffered(1)),
            pl.BlockSpec((tile, p_dim), prompt_idx),
            pl.BlockSpec((tile, p_dim), sample_idx, pipeline_mode=pl.Buffered(1)),
        ] + [_whole(w) for w in weights],
        out_specs=[
            pl.BlockSpec((tile, d_model), prompt_idx),
            pl.BlockSpec((tile, d_model), sample_idx, pipeline_mode=pl.Buffered(1)),
        ],
        out_shape=[
            jax.ShapeDtypeStruct((n_p, d_model), F32),
            jax.ShapeDtypeStruct((n_s, d_model), F32),
        ],
        compiler_params=pltpu.CompilerParams(
            dimension_semantics=("arbitrary",), vmem_limit_bytes=VMEM_LIMIT),
        name="ffn_ple",
    )(x_p, x_s, p_p, p_s, *weights)


def kernel(x_prompt, x_sample, state_pool, state_hgrn, p_prompt, p_sample, g_mix, w_in, w_pool_mix, pool_scale, hgrn_lb, hgrn_norm, w_pool_up, w_hgrn_up, w_out, g_ffn, w_ffn_gate, w_ffn_up, w_ffn_down, g_ple, w_ple_gate, w_ple_proj, g_final):
    depth = w_in.shape[0]
    assert depth == 1 and hgrn_lb.shape[0] == 2
    bsz, seq, d_model = x_prompt.shape
    dbsz, dseq, _ = x_sample.shape
    buf = state_pool.shape[2]

    mixer_w = (g_mix, _bf(w_in[0]), _bf(w_pool_mix[0]), pool_scale, hgrn_lb, hgrn_norm,
               _bf(w_pool_up[0]), _bf(w_hgrn_up[0]), _bf(w_out[0]))
    x1_p, pool_p, hgrn_p, (wg, wu, wd, wpg, wpp) = _mixer_prompt(
        x_prompt, mixer_w, (w_ffn_gate[0], w_ffn_up[0], w_ffn_down[0], w_ple_gate[0], w_ple_proj[0]),
        tile=512, chunk=128)
    ffn_w = (g_ffn, wg, wu, wd, g_ple, wpg, wpp, g_final.reshape(1, d_model))
    x1_s, pool_s, hgrn_s = _mixer_sample(x_sample, jnp.swapaxes(state_pool[0], 0, 1), state_hgrn[0],
                                         mixer_w, tile_b=16)

    y_p, y_s = _ffn_ple(x1_p.reshape(bsz * seq, d_model), x1_s,
                        p_prompt[0].reshape(bsz * seq, -1), p_sample[0].reshape(dbsz * dseq, -1),
                        ffn_w, tile=1024)

    return (y_p.reshape(bsz, seq, d_model), y_s.reshape(dbsz, dseq, d_model),
            pool_p[None, :, POOL_HALO - buf:, :], hgrn_p[None],
            jnp.swapaxes(pool_s, 0, 1)[None], hgrn_s[None])
```

```python
import functools

import jax
import jax.numpy as jnp
from jax import lax
from jax.experimental import pallas as pl
from jax.experimental.pallas import tpu as pltpu

F32 = jnp.float32
BF16 = jnp.bfloat16

EPS = 1e-6
POOL_WINDOWS = (2, 4, 8, 16)
assert POOL_WINDOWS == tuple(2 << g for g in range(len(POOL_WINDOWS)))
POOL_HALO = 16
POOL_PAD = 8
POOL_TOP = POOL_PAD + POOL_HALO
SUB = 16
LANES = 128
SUBLANES = 8
VMEM_LIMIT = 56 * 1024 * 1024
GATE_PIECE = 256
FFN_ROW_GROUPS = 2
FFN_PIECE = 768
SAMPLE_UNROLL = 4
PROMPT_STREAMS = 1

_NT = (((1,), (1,)), ((), ()))
_TN = (((0,), (0,)), ((), ()))


def _bf(x):
    return x.astype(BF16)


def _dot(a, b):
    return jnp.dot(a, b, preferred_element_type=F32)


def _rms(x, gain):
    ms = jnp.mean(x * x, axis=-1, keepdims=True)
    return x * lax.rsqrt(ms + EPS) * gain


def _silu(x):
    return x * jax.nn.sigmoid(x)


def _forget_lower_bound(lb_ref):
    a0 = lb_ref[0:1, :]
    a1 = lb_ref[1:2, :]
    m = jnp.maximum(a0, a1)
    e0 = jnp.exp(a0 - m)
    e1 = jnp.exp(a1 - m)
    return e0 / (e0 + e1)


def _scan_rows8(x, pos8):
    y = x.reshape(x.shape[0] // SUBLANES, SUBLANES, x.shape[1])
    s = 1
    while s < SUBLANES:
        y = y + jnp.where(pos8 >= s, pltpu.roll(y, s, axis=1), 0.0)
        s *= 2
    return y


def _last_row(y):
    return jnp.broadcast_to(y[..., SUBLANES - 1:SUBLANES, :], y.shape)


def _finish_head(a, v, s_prev, q_big, k_big, dec_row):
    vb = _bf(v)
    if a.shape[1] % LANES == 0:
        o = _dot(jnp.concatenate([_bf(a), _bf(q_big)], axis=1),
                 jnp.concatenate([vb, _bf(s_prev)], axis=0))
    else:
        o = _dot(_bf(a), vb) + _dot(_bf(q_big), _bf(s_prev))
    ds = lax.dot_general(_bf(k_big), vb, _TN, preferred_element_type=F32)
    dk = s_prev.shape[0]
    dec = jnp.transpose(jnp.broadcast_to(dec_row, (dk, dk)))
    return o, dec * s_prev + ds


def _scale_blocks(x, factors):
    out = []
    for i, f in enumerate(factors):
        blk = x[i * SUB:(i + 1) * SUB]
        out.append(blk if f is None else blk * jnp.concatenate([f] * (SUB // SUBLANES), axis=0))
    return jnp.concatenate(out, axis=0)


def _chunk_front(q, k, logf, pos8):
    n = q.shape[0]
    nb = n // SUB
    y = _scan_rows8(logf, pos8)
    cs_blk, tots = [], []
    for i in range(nb):
        lo = y[2 * i]
        hi = y[2 * i + 1] + _last_row(y[2 * i])
        cs_blk += [lo, hi]
        tots.append(_last_row(hi))
    cs = jnp.concatenate(cs_blk, axis=0)
    rs = jnp.concatenate([t for t in tots for _ in range(SUB // SUBLANES)], axis=0) - cs
    ecs = jnp.exp(cs)
    qe = q * ecs
    ke = k * jnp.exp(-cs)
    kd = k * jnp.exp(rs)

    def span(lo, hi):
        acc = None
        for m in range(lo, hi):
            acc = tots[m] if acc is None else acc + tots[m]
        return acc

    def expo(t):
        return None if t is None else jnp.exp(t)

    levels = []
    w = 2
    while w * SUB < n:
        qf = [expo(span((i // w) * w, i)) if (i // w) % 2 == 1 else None for i in range(nb)]
        kf = [expo(span(i + 1, (i // w + 1) * w)) if (i // w) % 2 == 0 else None for i in range(nb)]
        levels.append((_bf(_scale_blocks(qe, qf)), _bf(_scale_blocks(kd, kf))))
        w *= 2
    return {
        "q01": _bf(qe), "k01": _bf(jnp.concatenate([ke, kd], axis=0)),
        "levels": levels,
        "q_big": _bf(_scale_blocks(qe, [expo(span(0, i)) for i in range(nb)])),
        "k_big": _bf(_scale_blocks(kd, [expo(span(i + 1, nb)) for i in range(nb)])),
        "dec_row": jnp.exp(span(0, nb))[0:1, :],
    }


def _chunk_back(fr, v, s_prev, masks):
    n = fr["q01"].shape[0]
    a01 = lax.dot_general(fr["q01"], fr["k01"], _NT, preferred_element_type=F32)
    a = jnp.where(masks[0], a01[:, :n], jnp.where(masks[1], a01[:, n:], 0.0))
    for lvl, (q_l, k_l) in enumerate(fr["levels"]):
        a_l = lax.dot_general(q_l, k_l, _NT, preferred_element_type=F32)
        a = jnp.where(masks[lvl + 2], a_l, a)
    return _finish_head(a, v, s_prev, fr["q_big"], fr["k_big"], fr["dec_row"])


def _gated_merge_out(x, ya, ga, gb, o_raw, zg, hgrn_norm_ref, w_hgrn_up_ref, w_out_ref, n_heads):
    pieces = []
    for hh in range(n_heads):
        ln = slice(hh * LANES, (hh + 1) * LANES)
        oh = o_raw[:, ln]
        oh = oh * lax.rsqrt(jnp.mean(oh * oh, axis=-1, keepdims=True) + EPS)
        pieces.append(oh * hgrn_norm_ref[:, ln])
    o = jnp.concatenate(pieces, axis=-1) * _silu(zg)
    yb = _dot(_bf(o), w_hgrn_up_ref[...])
    merged = jax.nn.sigmoid(ga) * ya + jax.nn.sigmoid(gb) * yb
    return x + _dot(_bf(merged), w_out_ref[...])


def _pool_project(pooled, w_pool_mix_ref, pool_scale_ref, w_pool_up_ref):
    pieces = [_dot(_bf(pg), w_pool_mix_ref[g]) for g, pg in enumerate(pooled)]
    pool_out = jnp.concatenate(pieces, axis=-1) * pool_scale_ref[...]
    return _dot(_bf(pool_out), w_pool_up_ref[...])


def _prompt_stream(s, t, x_ref, g_mix_ref, w_in_ref, w_pool_mix_ref, pool_scale_ref, lb_ref,
                   hgrn_norm_ref, w_pool_up_ref, w_hgrn_up_ref, w_out_ref, x1_ref,
                   u_scr, z_scr, o_scr, s_scr, chunk, pos8, masks):
    _, tm, d_model = x_ref.shape
    pw = u_scr.shape[3]
    hw = o_scr.shape[2]
    n_heads = hw // LANES
    col_ga = pw + 4 * hw
    st = {"gates": [], "pooled": [None] * len(POOL_WINDOWS)}

    def norm():
        st["h"] = _bf(_rms(x_ref[s], g_mix_ref[...]))
        st["lb"] = _forget_lower_bound(lb_ref)

    def proj_u():
        u_scr[s, 0, POOL_TOP:POOL_TOP + tm, :] = _dot(st["h"], w_in_ref[:, 0:pw])

    def proj_z(c0, c1):
        def run():
            z_scr[s, :, c0:c1] = _dot(st["h"], w_in_ref[:, pw + c0:pw + c1])
        return run

    def window_level(lv):
        def run():
            shift = 1 << lv
            lanes = slice(lv * LANES, pw)
            src = 0 if lv == 0 else 1 + (lv - 1) % 2
            rows_all = POOL_HALO + tm
            sums = (u_scr[s, src, POOL_PAD:POOL_PAD + rows_all, lanes]
                    + u_scr[s, src, pl.ds(POOL_PAD - shift, rows_all), lanes])
            if lanes.start + LANES < pw:
                u_scr[s, 1 + lv % 2, POOL_PAD:POOL_PAD + rows_all, lanes] = sums
            seen = lax.broadcasted_iota(jnp.int32, (tm, 1), 0) + (t * tm + 1)
            inv_cnt = 1.0 / jnp.minimum(seen, 2 * shift).astype(F32)
            ug = u_scr[s, 0, POOL_TOP:POOL_TOP + tm, lv * LANES:(lv + 1) * LANES]
            st["pooled"][lv] = sums[POOL_HALO:, 0:LANES] * inv_cnt - ug
        return run

    def pool():
        st["ya"] = _pool_project(st["pooled"], w_pool_mix_ref, pool_scale_ref, w_pool_up_ref)
        u_scr[s, 0, POOL_PAD:POOL_TOP, :] = u_scr[s, 0, POOL_PAD + tm:POOL_TOP + tm, :]

    steps = [(c, hh) for c in range(tm // chunk) for hh in range(n_heads)]
    half = n_heads // 2 * LANES

    def front(k):
        c, hh = steps[k]

        def run():
            rows = slice(c * chunk, (c + 1) * chunk)
            zq = z_scr[s, rows, hh * LANES:(hh + 1) * LANES]
            zf = z_scr[s, rows, hw + hh * LANES:hw + (hh + 1) * LANES]
            lbh = st["lb"][:, hh * LANES:(hh + 1) * LANES]
            fg = lbh + (1.0 - lbh) * jax.nn.sigmoid(zf)
            st["front"][k] = _chunk_front(_silu(zq), 1.0 - fg, jnp.log(fg), pos8)
        return run

    def back(k):
        c, hh = steps[k]

        def run():
            rows = slice(c * chunk, (c + 1) * chunk)
            v = z_scr[s, rows, 2 * hw + hh * LANES:2 * hw + (hh + 1) * LANES]
            o, s_new = _chunk_back(st["front"][k], v, s_scr[s, hh], masks)
            s_scr[s, hh] = s_new
            o_scr[s, rows, hh * LANES:(hh + 1) * LANES] = o
        return run

    def gate(c0):
        def run():
            st["gates"].append(_dot(st["h"], w_in_ref[:, c0:c0 + GATE_PIECE]))
        return run

    def tail():
        gates = jnp.concatenate(st["gates"], axis=1)
        x1_ref[s] = _gated_merge_out(x_ref[s], st["ya"], gates[:, :d_model], gates[:, d_model:],
                                     o_scr[s], z_scr[s, :, 3 * hw:4 * hw], hgrn_norm_ref,
                                     w_hgrn_up_ref, w_out_ref, n_heads)

    st["front"] = [None] * len(steps)
    gate_cols = [(col_ga + j * GATE_PIECE) for j in range(2 * d_model // GATE_PIECE)]
    early = [k for k, (_, hh) in enumerate(steps) if hh < n_heads // 2]
    late = [k for k, (_, hh) in enumerate(steps) if hh >= n_heads // 2]
    return [
        [[norm]],
        [[proj_u, proj_z(0, half), proj_z(hw, hw + half)]],
        [[proj_z(half, hw), proj_z(hw + half, 2 * hw), proj_z(2 * hw, 3 * hw), proj_z(3 * hw, 4 * hw)],
         [front(k) for k in early + late], [window_level(lv) for lv in range(len(POOL_WINDOWS))]],
        [[back(k) for k in range(len(steps))], [gate(c0) for c0 in gate_cols] + [pool]],
        [[tail]],
    ]


def _merge_evenly(lists):
    keyed = [((i + 0.5) / len(lst), n, fn) for n, lst in enumerate(lists) for i, fn in enumerate(lst)]
    return [fn for _, _, fn in sorted(keyed, key=lambda e: e[:2])]


def _emit_staggered(streams):
    n_stages = len(streams[0])
    for slot in range(n_stages + len(streams) - 1):
        lists = []
        for s, stages in enumerate(streams):
            if 0 <= slot - s < n_stages:
                lists += stages[slot - s]
        for phase in _merge_evenly(lists):
            phase()


def _mixer_prompt_kernel(x_ref, g_mix_ref, w_in_ref, w_pool_mix_ref, pool_scale_ref, lb_ref,
                         hgrn_norm_ref, w_pool_up_ref, w_hgrn_up_ref, w_out_ref, *rest, chunk, n_cast):
    cast_in = rest[:n_cast]
    x1_ref, pool_out_ref, hgrn_out_ref = rest[n_cast:n_cast + 3]
    cast_out = rest[n_cast + 3:2 * n_cast + 3]
    u_scr, z_scr, o_scr, s_scr = rest[2 * n_cast + 3:]
    n_seq = x_ref.shape[0]
    t = pl.program_id(1)

    @pl.when(t == 0)
    def _():
        u_scr[:, :, 0:POOL_TOP, :] = jnp.zeros(u_scr.shape[:2] + (POOL_TOP, u_scr.shape[3]), F32)
        s_scr[...] = jnp.zeros(s_scr.shape, F32)

    pos8 = lax.broadcasted_iota(jnp.int32, (1, SUBLANES, LANES), 1)
    ti = lax.broadcasted_iota(jnp.int32, (chunk, chunk), 0)
    si = lax.broadcasted_iota(jnp.int32, (chunk, chunk), 1)
    masks = [((ti // SUB) == (si // SUB)) & (si <= ti)]
    b = SUB
    while b < chunk:
        masks.append(((ti // (2 * b)) == (si // (2 * b))) & ((ti & b) != 0) & ((si & b) == 0))
        b *= 2

    streams = [_prompt_stream(s, t, x_ref, g_mix_ref, w_in_ref, w_pool_mix_ref, pool_scale_ref, lb_ref,
                              hgrn_norm_ref, w_pool_up_ref, w_hgrn_up_ref, w_out_ref, x1_ref,
                              u_scr, z_scr, o_scr, s_scr, chunk, pos8, masks) for s in range(n_seq)]

    def cast_slabs():
        for src, dst in zip(cast_in, cast_out):
            dst[...] = _bf(src[...])

    streams[0][1].append([cast_slabs])
    _emit_staggered(streams)

    @pl.when(t == pl.num_programs(1) - 1)
    def _():
        pool_out_ref[...] = u_scr[:, 0, POOL_PAD:POOL_TOP, :]
        hgrn_out_ref[...] = s_scr[...]


def _mixer_sample_kernel(x_ref, sp_ref, sh_ref, g_mix_ref, w_in_ref, w_pool_mix_ref, pool_scale_ref,
                         lb_ref, hgrn_norm_ref, w_pool_up_ref, w_hgrn_up_ref, w_out_ref,
                         x1_ref, pool_out_ref, hgrn_out_ref,
                         e_scr, z_scr, o_scr, qe_scr, ke_scr, kd_scr, dec_scr, *, seq):
    rows_n, d_model = x_ref.shape
    buf, tb, pw = sp_ref.shape
    hw = o_scr.shape[1]
    n_heads = hw // LANES

    x = x_ref[...]
    h = _bf(_rms(x, g_mix_ref[...]))

    u = _dot(h, w_in_ref[:, 0:pw])
    pooled = []
    for g, w in enumerate(POOL_WINDOWS):
        ln = slice(g * LANES, (g + 1) * LANES)
        e_scr[g] = u[:, ln]
        ext = [sp_ref[e, :, ln] for e in range(buf)]
        ext += [e_scr[g, pl.ds(tt, tb, stride=seq), :] for tt in range(seq)]
        for e in range(buf):
            pool_out_ref[e, :, ln] = ext[seq + e]
        for tt in range(seq):
            acc = ext[buf + tt]
            for j in range(1, w):
                acc = acc + ext[buf + tt - j]
            e_scr[g, pl.ds(tt, tb, stride=seq), :] = acc * (1.0 / w) - ext[buf + tt]
        pooled.append(e_scr[g])
    col_ga = pw + 4 * hw
    ya = _pool_project(pooled, w_pool_mix_ref, pool_scale_ref, w_pool_up_ref)

    z_scr[...] = _dot(h, w_in_ref[:, pw:pw + 4 * hw])
    lb = _forget_lower_bound(lb_ref)
    pos8 = lax.broadcasted_iota(jnp.int32, (1, SUBLANES, LANES), 1)
    for hh in range(n_heads):
        ln = slice(hh * LANES, (hh + 1) * LANES)
        zq = z_scr[:, hh * LANES:(hh + 1) * LANES]
        zf = z_scr[:, hw + hh * LANES:hw + (hh + 1) * LANES]
        lbh = lb[:, ln]
        fg = lbh + (1.0 - lbh) * jax.nn.sigmoid(zf)
        k = 1.0 - fg
        q = _silu(zq)
        y = _scan_rows8(jnp.log(fg), pos8)
        tot = _last_row(y)
        cs = y.reshape(rows_n, LANES)
        qe_scr[:, ln] = q * jnp.exp(cs)
        ke_scr[:, ln] = k * jnp.exp(-cs)
        kd_scr[:, ln] = k * jnp.exp((tot - y).reshape(rows_n, LANES))
        dec_scr[:, ln] = jnp.exp(tot).reshape(rows_n, LANES)

    ti = lax.broadcasted_iota(jnp.int32, (seq, seq), 0)
    si = lax.broadcasted_iota(jnp.int32, (seq, seq), 1)
    causal = si <= ti

    def seq_body(b, carry):
        r0 = pl.multiple_of(b * seq, seq)
        rows = pl.ds(r0, seq)
        for hh in range(n_heads):
            ln = slice(hh * LANES, (hh + 1) * LANES)
            qe = qe_scr[rows, ln]
            v = z_scr[rows, 2 * hw + hh * LANES:2 * hw + (hh + 1) * LANES]
            a = lax.dot_general(_bf(qe), _bf(ke_scr[rows, ln]), _NT, preferred_element_type=F32)
            o, s_new = _finish_head(jnp.where(causal, a, 0.0), v, sh_ref[b, hh], qe,
                                    kd_scr[rows, ln], dec_scr[pl.ds(r0, 1), ln])
            hgrn_out_ref[b, hh] = s_new
            o_scr[rows, ln] = o
        return carry

    lax.fori_loop(0, tb, seq_body, 0, unroll=SAMPLE_UNROLL)

    gates = _dot(h, w_in_ref[:, col_ga:col_ga + 2 * d_model])
    x1_ref[...] = _gated_merge_out(x, ya, gates[:, :d_model], gates[:, d_model:], o_scr[...],
                                   z_scr[:, 3 * hw:4 * hw], hgrn_norm_ref, w_hgrn_up_ref, w_out_ref,
                                   n_heads)


def _ffn_ple_kernel(xp_ref, xs_ref, pp_ref, ps_ref, g_ffn_ref, w_gate_ref, w_up_ref, w_down_ref,
                    g_ple_ref, w_ple_gate_ref, w_ple_proj_ref, g_final_ref, yp_ref, ys_ref, *, n_prompt):
    d_ff = w_gate_ref.shape[1]
    ff_cuts = list(range(0, d_ff, FFN_PIECE)) + [d_ff]

    def row_group(x_ref, p_ref, y_ref, rows):
        st = {"act": []}

        def norm():
            st["x"] = x_ref[rows, :]
            st["h2"] = _bf(_rms(st["x"], g_ffn_ref[...]))

        def ff(c0, c1):
            def run():
                g = _dot(st["h2"], w_gate_ref[:, c0:c1])
                st["act"].append(_bf(_silu(g) * _dot(st["h2"], w_up_ref[:, c0:c1])))
            return run

        def down():
            x = st["x"] + _dot(jnp.concatenate(st["act"], axis=1), w_down_ref[...])
            st["x"] = x
            st["h3"] = _bf(_rms(x, g_ple_ref[...]))

        def ple():
            gate = jax.nn.sigmoid(_dot(st["h3"], w_ple_gate_ref[...]))
            emb = _dot(_bf(p_ref[rows, :]), w_ple_proj_ref[...])
            y_ref[rows, :] = _rms(st["x"] + gate * emb, g_final_ref[...])

        return [[[norm]], [[ff(c0, c1) for c0, c1 in zip(ff_cuts, ff_cuts[1:])]], [[down]], [[ple]]]

    def tile(x_ref, p_ref, y_ref):
        n = x_ref.shape[0] // FFN_ROW_GROUPS
        _emit_staggered([row_group(x_ref, p_ref, y_ref, slice(r * n, (r + 1) * n))
                         for r in range(FFN_ROW_GROUPS)])

    i = pl.program_id(0)

    @pl.when(i < n_prompt)
    def _():
        tile(xp_ref, pp_ref, yp_ref)

    @pl.when(i >= n_prompt)
    def _():
        tile(xs_ref, ps_ref, ys_ref)


def _whole(_):
    return pl.BlockSpec(memory_space=pltpu.VMEM)


def _mixer_prompt(x, weights, to_cast, *, tile, chunk):
    bsz, seq, d_model = x.shape
    pw = weights[3].shape[1]
    hw = weights[5].shape[1]
    n_heads = hw // LANES
    ns = PROMPT_STREAMS
    assert seq % tile == 0 and tile % chunk == 0 and chunk % SUB == 0 and bsz % ns == 0
    nt = seq // tile
    n_steps = (bsz // ns) * nt
    slabs = []
    for w in to_cast:
        rep = 1 if w.shape[0] % (n_steps * 2 * SUBLANES) == 0 else 2
        rows = w.shape[0] * rep // n_steps
        assert rows * n_steps == w.shape[0] * rep and rows % (2 * SUBLANES) == 0
        slabs.append(pl.BlockSpec((rows, w.shape[1]), lambda b, t, rep=rep: ((b * nt + t) // rep, 0)))
    outs = pl.pallas_call(
        functools.partial(_mixer_prompt_kernel, chunk=chunk, n_cast=len(to_cast)),
        grid=(bsz // ns, nt),
        in_specs=([pl.BlockSpec((ns, tile, d_model), lambda b, t: (b, t, 0))]
                  + [_whole(w) for w in weights] + slabs),
        out_specs=[
            pl.BlockSpec((ns, tile, d_model), lambda b, t: (b, t, 0)),
            pl.BlockSpec((ns, POOL_HALO, pw), lambda b, t: (b, 0, 0)),
            pl.BlockSpec((ns, n_heads, LANES, LANES), lambda b, t: (b, 0, 0, 0)),
        ] + slabs,
        out_shape=[
            jax.ShapeDtypeStruct((bsz, seq, d_model), F32),
            jax.ShapeDtypeStruct((bsz, POOL_HALO, pw), F32),
            jax.ShapeDtypeStruct((bsz, n_heads, LANES, LANES), F32),
        ] + [jax.ShapeDtypeStruct(w.shape, BF16) for w in to_cast],
        scratch_shapes=[
            pltpu.VMEM((ns, 3, POOL_TOP + tile, pw), F32),
            pltpu.VMEM((ns, tile, 4 * hw), F32),
            pltpu.VMEM((ns, tile, hw), F32),
            pltpu.VMEM((ns, n_heads, LANES, LANES), F32),
        ],
        compiler_params=pltpu.CompilerParams(
            dimension_semantics=("parallel", "arbitrary"), vmem_limit_bytes=VMEM_LIMIT),
        name="mixer_prompt",
    )(x, *weights, *to_cast)
    return outs[0], outs[1], outs[2], outs[3:]


def _mixer_sample(x, state_pool, state_hgrn, weights, *, tile_b):
    bsz, seq, d_model = x.shape
    buf, _, pw = state_pool.shape
    _, n_heads, dk, dv = state_hgrn.shape
    hw = n_heads * dv
    assert bsz % tile_b == 0 and seq == SUBLANES and buf >= max(POOL_WINDOWS) - 1
    assert dk == LANES and dv == LANES
    rows = tile_b * seq
    xf = x.reshape(bsz * seq, d_model)
    return pl.pallas_call(
        functools.partial(_mixer_sample_kernel, seq=seq),
        grid=(bsz // tile_b,),
        in_specs=[
            pl.BlockSpec((rows, d_model), lambda i: (i, 0)),
            pl.BlockSpec((buf, tile_b, pw), lambda i: (0, i, 0)),
            pl.BlockSpec((tile_b, n_heads, dk, dv), lambda i: (i, 0, 0, 0)),
        ] + [_whole(w) for w in weights],
        out_specs=[
            pl.BlockSpec((rows, d_model), lambda i: (i, 0)),
            pl.BlockSpec((buf, tile_b, pw), lambda i: (0, i, 0)),
            pl.BlockSpec((tile_b, n_heads, dk, dv), lambda i: (i, 0, 0, 0)),
        ],
        out_shape=[
            jax.ShapeDtypeStruct((bsz * seq, d_model), F32),
            jax.ShapeDtypeStruct((buf, bsz, pw), F32),
            jax.ShapeDtypeStruct((bsz, n_heads, dk, dv), F32),
        ],
        scratch_shapes=[
            pltpu.VMEM((pw // LANES, rows, LANES), F32),
            pltpu.VMEM((rows, 4 * hw), F32),
            pltpu.VMEM((rows, hw), F32),
            pltpu.VMEM((rows, hw), F32),
            pltpu.VMEM((rows, hw), F32),
            pltpu.VMEM((rows, hw), F32),
            pltpu.VMEM((rows, hw), F32),
        ],
        compiler_params=pltpu.CompilerParams(
            dimension_semantics=("parallel",), vmem_limit_bytes=VMEM_LIMIT),
        name="mixer_sample",
    )(xf, state_pool, state_hgrn, *weights)


def _ffn_ple(x_p, x_s, p_p, p_s, weights, *, tile):
    (n_p, d_model), n_s = x_p.shape, x_s.shape[0]
    p_dim = p_p.shape[1]
    assert n_p % tile == 0 and n_s % tile == 0
    tp, ts = n_p // tile, n_s // tile

    def prompt_idx(i):
        return (jnp.minimum(i, tp - 1), 0)

    def sample_idx(i):
        return (jnp.maximum(i - tp, 0), 0)

    return pl.pallas_call(
        functools.partial(_ffn_ple_kernel, n_prompt=tp),
        grid=(tp + ts,),
        in_specs=[
            pl.BlockSpec((tile, d_model), prompt_idx),
            pl.BlockSpec((tile, d_model), sample_idx),
            pl.BlockSpec((tile, p_dim), prompt_idx),
            pl.BlockSpec((tile, p_dim), sample_idx),
        ] + [_whole(w) for w in weights],
        out_specs=[
            pl.BlockSpec((tile, d_model), prompt_idx),
            pl.BlockSpec((tile, d_model), sample_idx),
        ],
        out_shape=[
            jax.ShapeDtypeStruct((n_p, d_model), F32),
            jax.ShapeDtypeStruct((n_s, d_model), F32),
        ],
        compiler_params=pltpu.CompilerParams(
            dimension_semantics=("arbitrary",), vmem_limit_bytes=VMEM_LIMIT),
        name="ffn_ple",
    )(x_p, x_s, p_p, p_s, *weights)


def kernel(x_prompt, x_sample, state_pool, state_hgrn, p_prompt, p_sample, g_mix, w_in, w_pool_mix, pool_scale, hgrn_lb, hgrn_norm, w_pool_up, w_hgrn_up, w_out, g_ffn, w_ffn_gate, w_ffn_up, w_ffn_down, g_ple, w_ple_gate, w_ple_proj, g_final):
    depth = w_in.shape[0]
    assert depth == 1 and hgrn_lb.shape[0] == 2
    bsz, seq, d_model = x_prompt.shape
    dbsz, dseq, _ = x_sample.shape
    buf = state_pool.shape[2]

    mixer_w = (g_mix, _bf(w_in[0]), _bf(w_pool_mix[0]), pool_scale, hgrn_lb, hgrn_norm,
               _bf(w_pool_up[0]), _bf(w_hgrn_up[0]), _bf(w_out[0]))
    x1_p, pool_p, hgrn_p, (wg, wu, wd, wpg, wpp) = _mixer_prompt(
        x_prompt, mixer_w, (w_ffn_gate[0], w_ffn_up[0], w_ffn_down[0], w_ple_gate[0], w_ple_proj[0]),
        tile=512, chunk=128)
    ffn_w = (g_ffn, wg, wu, wd, g_ple, wpg, wpp, g_final.reshape(1, d_model))
    x1_s, pool_s, hgrn_s = _mixer_sample(x_sample, jnp.swapaxes(state_pool[0], 0, 1), state_hgrn[0],
                                         mixer_w, tile_b=16)

    y_p, y_s = _ffn_ple(x1_p.reshape(bsz * seq, d_model), x1_s,
                        p_prompt[0].reshape(bsz * seq, -1), p_sample[0].reshape(dbsz * dseq, -1),
                        ffn_w, tile=512)

    return (y_p.reshape(bsz, seq, d_model), y_s.reshape(dbsz, dseq, d_model),
            pool_p[None, :, POOL_HALO - buf:, :], hgrn_p[None],
            jnp.swapaxes(pool_s, 0, 1)[None], hgrn_s[None])
```

```python
import functools

import jax
import jax.numpy as jnp
from jax import lax
from jax.experimental import pallas as pl
from jax.experimental.pallas import tpu as pltpu

F32 = jnp.float32
BF16 = jnp.bfloat16

EPS = 1e-6
POOL_WINDOWS = (2, 4, 8, 16)
assert POOL_WINDOWS == tuple(2 << g for g in range(len(POOL_WINDOWS)))
POOL_HALO = 16
POOL_PAD = 8
POOL_TOP = POOL_PAD + POOL_HALO
SUB = 16
LANES = 128
SUBLANES = 8
VMEM_LIMIT = 56 * 1024 * 1024
GATE_PIECE = 256
FFN_ROW_GROUPS = 2
FFN_PIECE = 768
N_MIXER_WEIGHTS = 5
STAGE_ROWS = 128
SAMPLE_UNROLL = 4
PROMPT_STREAMS = 1

_NT = (((1,), (1,)), ((), ()))
_TN = (((0,), (0,)), ((), ()))


def _bf(x):
    return x.astype(BF16)


def _dot(a, b):
    return jnp.dot(a, b, preferred_element_type=F32)


def _rms(x, gain):
    ms = jnp.mean(x * x, axis=-1, keepdims=True)
    return x * lax.rsqrt(ms + EPS) * gain


def _silu(x):
    return x * jax.nn.sigmoid(x)


def _forget_lower_bound(lb_ref):
    a0 = lb_ref[0:1, :]
    a1 = lb_ref[1:2, :]
    m = jnp.maximum(a0, a1)
    e0 = jnp.exp(a0 - m)
    e1 = jnp.exp(a1 - m)
    return e0 / (e0 + e1)


def _scan_rows8(x, pos8):
    y = x.reshape(x.shape[0] // SUBLANES, SUBLANES, x.shape[1])
    s = 1
    while s < SUBLANES:
        y = y + jnp.where(pos8 >= s, pltpu.roll(y, s, axis=1), 0.0)
        s *= 2
    return y


def _last_row(y):
    return jnp.broadcast_to(y[..., SUBLANES - 1:SUBLANES, :], y.shape)


def _finish_head(a, v, s_prev, q_big, k_big, dec_row):
    vb = _bf(v)
    if a.shape[1] % LANES == 0:
        o = _dot(jnp.concatenate([_bf(a), _bf(q_big)], axis=1),
                 jnp.concatenate([vb, _bf(s_prev)], axis=0))
    else:
        o = _dot(_bf(a), vb) + _dot(_bf(q_big), _bf(s_prev))
    ds = lax.dot_general(_bf(k_big), vb, _TN, preferred_element_type=F32)
    dk = s_prev.shape[0]
    dec = jnp.transpose(jnp.broadcast_to(dec_row, (dk, dk)))
    return o, dec * s_prev + ds


def _scale_blocks(x, factors):
    out = []
    for i, f in enumerate(factors):
        blk = x[i * SUB:(i + 1) * SUB]
        out.append(blk if f is None else blk * jnp.concatenate([f] * (SUB // SUBLANES), axis=0))
    return jnp.concatenate(out, axis=0)


def _chunk_front(q, k, logf, pos8):
    n = q.shape[0]
    nb = n // SUB
    y = _scan_rows8(logf, pos8)
    cs_blk, tots = [], []
    for i in range(nb):
        lo = y[2 * i]
        hi = y[2 * i + 1] + _last_row(y[2 * i])
        cs_blk += [lo, hi]
        tots.append(_last_row(hi))
    cs = jnp.concatenate(cs_blk, axis=0)
    rs = jnp.concatenate([t for t in tots for _ in range(SUB // SUBLANES)], axis=0) - cs
    ecs = jnp.exp(cs)
    qe = q * ecs
    ke = k * jnp.exp(-cs)
    kd = k * jnp.exp(rs)

    def span(lo, hi):
        acc = None
        for m in range(lo, hi):
            acc = tots[m] if acc is None else acc + tots[m]
        return acc

    def expo(t):
        return None if t is None else jnp.exp(t)

    levels = []
    w = 2
    while w * SUB < n:
        qf = [expo(span((i // w) * w, i)) if (i // w) % 2 == 1 else None for i in range(nb)]
        kf = [expo(span(i + 1, (i // w + 1) * w)) if (i // w) % 2 == 0 else None for i in range(nb)]
        levels.append((_bf(_scale_blocks(qe, qf)), _bf(_scale_blocks(kd, kf))))
        w *= 2
    return {
        "q01": _bf(qe), "k01": _bf(jnp.concatenate([ke, kd], axis=0)),
        "levels": levels,
        "q_big": _bf(_scale_blocks(qe, [expo(span(0, i)) for i in range(nb)])),
        "k_big": _bf(_scale_blocks(kd, [expo(span(i + 1, nb)) for i in range(nb)])),
        "dec_row": jnp.exp(span(0, nb))[0:1, :],
    }


def _chunk_back(fr, v, s_prev, masks):
    n = fr["q01"].shape[0]
    a01 = lax.dot_general(fr["q01"], fr["k01"], _NT, preferred_element_type=F32)
    a = jnp.where(masks[0], a01[:, :n], jnp.where(masks[1], a01[:, n:], 0.0))
    for lvl, (q_l, k_l) in enumerate(fr["levels"]):
        a_l = lax.dot_general(q_l, k_l, _NT, preferred_element_type=F32)
        a = jnp.where(masks[lvl + 2], a_l, a)
    return _finish_head(a, v, s_prev, fr["q_big"], fr["k_big"], fr["dec_row"])


def _gated_merge_out(x, ya, ga, gb, o_raw, zg, hgrn_norm_ref, w_hgrn_up_ref, w_out_ref, n_heads):
    pieces = []
    for hh in range(n_heads):
        ln = slice(hh * LANES, (hh + 1) * LANES)
        oh = o_raw[:, ln]
        oh = oh * lax.rsqrt(jnp.mean(oh * oh, axis=-1, keepdims=True) + EPS)
        pieces.append(oh * hgrn_norm_ref[:, ln])
    o = jnp.concatenate(pieces, axis=-1) * _silu(zg)
    yb = _dot(_bf(o), w_hgrn_up_ref[...])
    merged = jax.nn.sigmoid(ga) * ya + jax.nn.sigmoid(gb) * yb
    return x + _dot(_bf(merged), w_out_ref[...])


def _pool_project(pooled, w_pool_mix_ref, pool_scale_ref, w_pool_up_ref):
    pieces = [_dot(_bf(pg), w_pool_mix_ref[g * LANES:(g + 1) * LANES, :]) for g, pg in enumerate(pooled)]
    pool_out = jnp.concatenate(pieces, axis=-1) * pool_scale_ref[...]
    return _dot(_bf(pool_out), w_pool_up_ref[...])


def _prompt_stream(s, t, x_ref, g_mix_ref, w_in_ref, w_pool_mix_ref, pool_scale_ref, lb_ref,
                   hgrn_norm_ref, w_pool_up_ref, w_hgrn_up_ref, w_out_ref, x1_ref,
                   u_scr, z_scr, o_scr, s_scr, chunk, pos8, masks):
    _, tm, d_model = x_ref.shape
    pw = u_scr.shape[3]
    hw = o_scr.shape[2]
    n_heads = hw // LANES
    col_ga = pw + 4 * hw
    st = {"gates": [], "pooled": [None] * len(POOL_WINDOWS)}

    def norm():
        st["h"] = _bf(_rms(x_ref[s], g_mix_ref[...]))
        st["lb"] = _forget_lower_bound(lb_ref)

    def proj_u():
        u_scr[s, 0, POOL_TOP:POOL_TOP + tm, :] = _dot(st["h"], w_in_ref[:, 0:pw])

    def proj_z(c0, c1):
        def run():
            z_scr[s, :, c0:c1] = _dot(st["h"], w_in_ref[:, pw + c0:pw + c1])
        return run

    def window_level(lv):
        def run():
            shift = 1 << lv
            lanes = slice(lv * LANES, pw)
            src = 0 if lv == 0 else 1 + (lv - 1) % 2
            rows_all = POOL_HALO + tm
            sums = (u_scr[s, src, POOL_PAD:POOL_PAD + rows_all, lanes]
                    + u_scr[s, src, pl.ds(POOL_PAD - shift, rows_all), lanes])
            if lanes.start + LANES < pw:
                u_scr[s, 1 + lv % 2, POOL_PAD:POOL_PAD + rows_all, lanes] = sums
            seen = lax.broadcasted_iota(jnp.int32, (tm, 1), 0) + (t * tm + 1)
            inv_cnt = 1.0 / jnp.minimum(seen, 2 * shift).astype(F32)
            ug = u_scr[s, 0, POOL_TOP:POOL_TOP + tm, lv * LANES:(lv + 1) * LANES]
            st["pooled"][lv] = sums[POOL_HALO:, 0:LANES] * inv_cnt - ug
        return run

    def pool():
        st["ya"] = _pool_project(st["pooled"], w_pool_mix_ref, pool_scale_ref, w_pool_up_ref)
        u_scr[s, 0, POOL_PAD:POOL_TOP, :] = u_scr[s, 0, POOL_PAD + tm:POOL_TOP + tm, :]

    steps = [(c, hh) for c in range(tm // chunk) for hh in range(n_heads)]
    half = n_heads // 2 * LANES

    def front(k):
        c, hh = steps[k]

        def run():
            rows = slice(c * chunk, (c + 1) * chunk)
            zq = z_scr[s, rows, hh * LANES:(hh + 1) * LANES]
            zf = z_scr[s, rows, hw + hh * LANES:hw + (hh + 1) * LANES]
            lbh = st["lb"][:, hh * LANES:(hh + 1) * LANES]
            fg = lbh + (1.0 - lbh) * jax.nn.sigmoid(zf)
            st["front"][k] = _chunk_front(_silu(zq), 1.0 - fg, jnp.log(fg), pos8)
        return run

    def back(k):
        c, hh = steps[k]

        def run():
            rows = slice(c * chunk, (c + 1) * chunk)
            v = z_scr[s, rows, 2 * hw + hh * LANES:2 * hw + (hh + 1) * LANES]
            o, s_new = _chunk_back(st["front"][k], v, s_scr[s, hh], masks)
            s_scr[s, hh] = s_new
            o_scr[s, rows, hh * LANES:(hh + 1) * LANES] = o
        return run

    def gate(c0):
        def run():
            st["gates"].append(_dot(st["h"], w_in_ref[:, c0:c0 + GATE_PIECE]))
        return run

    def tail():
        gates = jnp.concatenate(st["gates"], axis=1)
        x1_ref[s] = _gated_merge_out(x_ref[s], st["ya"], gates[:, :d_model], gates[:, d_model:],
                                     o_scr[s], z_scr[s, :, 3 * hw:4 * hw], hgrn_norm_ref,
                                     w_hgrn_up_ref, w_out_ref, n_heads)

    st["front"] = [None] * len(steps)
    gate_cols = [(col_ga + j * GATE_PIECE) for j in range(2 * d_model // GATE_PIECE)]
    early = [k for k, (_, hh) in enumerate(steps) if hh < n_heads // 2]
    late = [k for k, (_, hh) in enumerate(steps) if hh >= n_heads // 2]
    return [
        [[norm]],
        [[proj_u, proj_z(0, half), proj_z(hw, hw + half)]],
        [[proj_z(half, hw), proj_z(hw + half, 2 * hw), proj_z(2 * hw, 3 * hw), proj_z(3 * hw, 4 * hw)],
         [front(k) for k in early + late], [window_level(lv) for lv in range(len(POOL_WINDOWS))]],
        [[back(k) for k in range(len(steps))], [gate(c0) for c0 in gate_cols] + [pool]],
        [[tail]],
    ]


def _merge_evenly(lists):
    keyed = [((i + 0.5) / len(lst), n, fn) for n, lst in enumerate(lists) for i, fn in enumerate(lst)]
    return [fn for _, _, fn in sorted(keyed, key=lambda e: e[:2])]


def _emit_staggered(streams):
    n_stages = len(streams[0])
    for slot in range(n_stages + len(streams) - 1):
        lists = []
        for s, stages in enumerate(streams):
            if 0 <= slot - s < n_stages:
                lists += stages[slot - s]
        for phase in _merge_evenly(lists):
            phase()


def _stage_weights(hbm_refs, bf_refs, stage, sem):
    rows = stage.shape[1]
    jobs = [(src, dst, k) for src, dst in zip(hbm_refs, bf_refs) for k in range(src.shape[0] // rows)]

    def chunk_copy(j):
        src, _, k = jobs[j]
        return pltpu.make_async_copy(src.at[pl.ds(k * rows, rows), :],
                                     stage.at[j % 2, :, pl.ds(0, src.shape[1])], sem.at[j % 2])

    chunk_copy(0).start()
    for j, (src, dst, k) in enumerate(jobs):
        if j + 1 < len(jobs):
            chunk_copy(j + 1).start()
        chunk_copy(j).wait()
        dst[k * rows:(k + 1) * rows, :] = _bf(stage[j % 2, :, 0:src.shape[1]])


def _mixer_prompt_kernel(x_ref, g_mix_ref, pool_scale_ref, lb_ref, hgrn_norm_ref, *rest,
                         chunk, n_cast):
    n_w = N_MIXER_WEIGHTS
    w_hbm = rest[:n_w]
    cast_in = rest[n_w:n_w + n_cast]
    x1_ref, pool_out_ref, hgrn_out_ref = rest[n_w + n_cast:n_w + n_cast + 3]
    w_out_slabs = rest[n_w + n_cast + 3:2 * n_w + n_cast + 3]
    cast_out = rest[2 * n_w + n_cast + 3:2 * n_w + 2 * n_cast + 3]
    w_bf = rest[2 * n_w + 2 * n_cast + 3:3 * n_w + 2 * n_cast + 3]
    stage, sem, u_scr, z_scr, o_scr, s_scr = rest[3 * n_w + 2 * n_cast + 3:]
    w_in_ref, w_pool_mix_ref, w_pool_up_ref, w_hgrn_up_ref, w_out_ref = w_bf
    n_seq = x_ref.shape[0]
    t = pl.program_id(1)
    step = pl.program_id(0) * pl.num_programs(1) + t

    @pl.when(step == 0)
    def _():
        _stage_weights(w_hbm, w_bf, stage, sem)

    @pl.when(t == 0)
    def _():
        u_scr[:, :, 0:POOL_TOP, :] = jnp.zeros(u_scr.shape[:2] + (POOL_TOP, u_scr.shape[3]), F32)
        s_scr[...] = jnp.zeros(s_scr.shape, F32)

    pos8 = lax.broadcasted_iota(jnp.int32, (1, SUBLANES, LANES), 1)
    ti = lax.broadcasted_iota(jnp.int32, (chunk, chunk), 0)
    si = lax.broadcasted_iota(jnp.int32, (chunk, chunk), 1)
    masks = [((ti // SUB) == (si // SUB)) & (si <= ti)]
    b = SUB
    while b < chunk:
        masks.append(((ti // (2 * b)) == (si // (2 * b))) & ((ti & b) != 0) & ((si & b) == 0))
        b *= 2

    streams = [_prompt_stream(s, t, x_ref, g_mix_ref, w_in_ref, w_pool_mix_ref, pool_scale_ref, lb_ref,
                              hgrn_norm_ref, w_pool_up_ref, w_hgrn_up_ref, w_out_ref, x1_ref,
                              u_scr, z_scr, o_scr, s_scr, chunk, pos8, masks) for s in range(n_seq)]

    def cast_slabs():
        for src, dst in zip(cast_in, cast_out):
            dst[...] = _bf(src[...])
        for src, dst in zip(w_bf, w_out_slabs):
            rows = dst.shape[0]
            dst[...] = src[pl.ds(pl.multiple_of(step * rows, rows), rows), :]

    streams[0][1].append([cast_slabs])
    _emit_staggered(streams)

    @pl.when(t == pl.num_programs(1) - 1)
    def _():
        pool_out_ref[...] = u_scr[:, 0, POOL_PAD:POOL_TOP, :]
        hgrn_out_ref[...] = s_scr[...]


def _mixer_sample_kernel(x_ref, sp_ref, sh_ref, g_mix_ref, w_in_ref, w_pool_mix_ref, pool_scale_ref,
                         lb_ref, hgrn_norm_ref, w_pool_up_ref, w_hgrn_up_ref, w_out_ref,
                         x1_ref, pool_out_ref, hgrn_out_ref,
                         e_scr, z_scr, o_scr, qe_scr, ke_scr, kd_scr, dec_scr, *, seq):
    rows_n, d_model = x_ref.shape
    buf, tb, pw = sp_ref.shape
    hw = o_scr.shape[1]
    n_heads = hw // LANES

    x = x_ref[...]
    h = _bf(_rms(x, g_mix_ref[...]))

    u = _dot(h, w_in_ref[:, 0:pw])
    pooled = []
    for g, w in enumerate(POOL_WINDOWS):
        ln = slice(g * LANES, (g + 1) * LANES)
        e_scr[g] = u[:, ln]
        ext = [sp_ref[e, :, ln] for e in range(buf)]
        ext += [e_scr[g, pl.ds(tt, tb, stride=seq), :] for tt in range(seq)]
        for e in range(buf):
            pool_out_ref[e, :, ln] = ext[seq + e]
        for tt in range(seq):
            acc = ext[buf + tt]
            for j in range(1, w):
                acc = acc + ext[buf + tt - j]
            e_scr[g, pl.ds(tt, tb, stride=seq), :] = acc * (1.0 / w) - ext[buf + tt]
        pooled.append(e_scr[g])
    col_ga = pw + 4 * hw
    ya = _pool_project(pooled, w_pool_mix_ref, pool_scale_ref, w_pool_up_ref)

    z_scr[...] = _dot(h, w_in_ref[:, pw:pw + 4 * hw])
    lb = _forget_lower_bound(lb_ref)
    pos8 = lax.broadcasted_iota(jnp.int32, (1, SUBLANES, LANES), 1)
    for hh in range(n_heads):
        ln = slice(hh * LANES, (hh + 1) * LANES)
        zq = z_scr[:, hh * LANES:(hh + 1) * LANES]
        zf = z_scr[:, hw + hh * LANES:hw + (hh + 1) * LANES]
        lbh = lb[:, ln]
        fg = lbh + (1.0 - lbh) * jax.nn.sigmoid(zf)
        k = 1.0 - fg
        q = _silu(zq)
        y = _scan_rows8(jnp.log(fg), pos8)
        tot = _last_row(y)
        cs = y.reshape(rows_n, LANES)
        qe_scr[:, ln] = q * jnp.exp(cs)
        ke_scr[:, ln] = k * jnp.exp(-cs)
        kd_scr[:, ln] = k * jnp.exp((tot - y).reshape(rows_n, LANES))
        dec_scr[:, ln] = jnp.exp(tot).reshape(rows_n, LANES)

    ti = lax.broadcasted_iota(jnp.int32, (seq, seq), 0)
    si = lax.broadcasted_iota(jnp.int32, (seq, seq), 1)
    causal = si <= ti

    def seq_body(b, carry):
        r0 = pl.multiple_of(b * seq, seq)
        rows = pl.ds(r0, seq)
        for hh in range(n_heads):
            ln = slice(hh * LANES, (hh + 1) * LANES)
            qe = qe_scr[rows, ln]
            v = z_scr[rows, 2 * hw + hh * LANES:2 * hw + (hh + 1) * LANES]
            a = lax.dot_general(_bf(qe), _bf(ke_scr[rows, ln]), _NT, preferred_element_type=F32)
            o, s_new = _finish_head(jnp.where(causal, a, 0.0), v, sh_ref[b, hh], qe,
                                    kd_scr[rows, ln], dec_scr[pl.ds(r0, 1), ln])
            hgrn_out_ref[b, hh] = s_new
            o_scr[rows, ln] = o
        return carry

    lax.fori_loop(0, tb, seq_body, 0, unroll=SAMPLE_UNROLL)

    gates = _dot(h, w_in_ref[:, col_ga:col_ga + 2 * d_model])
    x1_ref[...] = _gated_merge_out(x, ya, gates[:, :d_model], gates[:, d_model:], o_scr[...],
                                   z_scr[:, 3 * hw:4 * hw], hgrn_norm_ref, w_hgrn_up_ref, w_out_ref,
                                   n_heads)


def _ffn_ple_kernel(xp_ref, xs_ref, pp_ref, ps_ref, g_ffn_ref, w_gate_ref, w_up_ref, w_down_ref,
                    g_ple_ref, w_ple_gate_ref, w_ple_proj_ref, g_final_ref, yp_ref, ys_ref, *, n_prompt):
    d_ff = w_gate_ref.shape[1]
    ff_cuts = list(range(0, d_ff, FFN_PIECE)) + [d_ff]

    def row_group(x_ref, p_ref, y_ref, rows):
        st = {"act": []}

        def norm():
            st["x"] = x_ref[rows, :]
            st["h2"] = _bf(_rms(st["x"], g_ffn_ref[...]))

        def ff(c0, c1):
            def run():
                g = _dot(st["h2"], w_gate_ref[:, c0:c1])
                st["act"].append(_bf(_silu(g) * _dot(st["h2"], w_up_ref[:, c0:c1])))
            return run

        def down():
            x = st["x"] + _dot(jnp.concatenate(st["act"], axis=1), w_down_ref[...])
            st["x"] = x
            st["h3"] = _bf(_rms(x, g_ple_ref[...]))

        def ple():
            gate = jax.nn.sigmoid(_dot(st["h3"], w_ple_gate_ref[...]))
            emb = _dot(_bf(p_ref[rows, :]), w_ple_proj_ref[...])
            y_ref[rows, :] = _rms(st["x"] + gate * emb, g_final_ref[...])

        return [[[norm]], [[ff(c0, c1) for c0, c1 in zip(ff_cuts, ff_cuts[1:])]], [[down]], [[ple]]]

    def tile(x_ref, p_ref, y_ref):
        n = x_ref.shape[0] // FFN_ROW_GROUPS
        _emit_staggered([row_group(x_ref, p_ref, y_ref, slice(r * n, (r + 1) * n))
                         for r in range(FFN_ROW_GROUPS)])

    i = pl.program_id(0)

    @pl.when(i < n_prompt)
    def _():
        tile(xp_ref, pp_ref, yp_ref)

    @pl.when(i >= n_prompt)
    def _():
        tile(xs_ref, ps_ref, ys_ref)


def _whole(_):
    return pl.BlockSpec(memory_space=pltpu.VMEM)


def _mixer_prompt(x, vectors, matrices, to_cast, *, tile, chunk):
    bsz, seq, d_model = x.shape
    pw = vectors[1].shape[1]
    hw = vectors[3].shape[1]
    n_heads = hw // LANES
    ns = PROMPT_STREAMS
    assert len(matrices) == N_MIXER_WEIGHTS
    assert seq % tile == 0 and tile % chunk == 0 and chunk % SUB == 0 and bsz % ns == 0
    nt = seq // tile
    n_steps = (bsz // ns) * nt

    def row_slab(w, exact):
        rep = 1 if w.shape[0] % (n_steps * 2 * SUBLANES) == 0 else 2
        rows = w.shape[0] * rep // n_steps
        assert rows * n_steps == w.shape[0] * rep and rows % (2 * SUBLANES) == 0 and (rep == 1 or not exact)
        return pl.BlockSpec((rows, w.shape[1]), lambda b, t, rep=rep: ((b * nt + t) // rep, 0))

    slabs = [row_slab(w, False) for w in to_cast]
    w_slabs = [row_slab(w, True) for w in matrices]
    stage_cols = max(w.shape[1] for w in matrices)
    assert all(w.shape[0] % STAGE_ROWS == 0 for w in matrices)
    outs = pl.pallas_call(
        functools.partial(_mixer_prompt_kernel, chunk=chunk, n_cast=len(to_cast)),
        grid=(bsz // ns, nt),
        in_specs=([pl.BlockSpec((ns, tile, d_model), lambda b, t: (b, t, 0))]
                  + [_whole(v) for v in vectors]
                  + [pl.BlockSpec(memory_space=pl.ANY) for _ in matrices] + slabs),
        out_specs=[
            pl.BlockSpec((ns, tile, d_model), lambda b, t: (b, t, 0)),
            pl.BlockSpec((ns, POOL_HALO, pw), lambda b, t: (b, 0, 0)),
            pl.BlockSpec((ns, n_heads, LANES, LANES), lambda b, t: (b, 0, 0, 0)),
        ] + w_slabs + slabs,
        out_shape=[
            jax.ShapeDtypeStruct((bsz, seq, d_model), F32),
            jax.ShapeDtypeStruct((bsz, POOL_HALO, pw), F32),
            jax.ShapeDtypeStruct((bsz, n_heads, LANES, LANES), F32),
        ] + [jax.ShapeDtypeStruct(w.shape, BF16) for w in matrices + to_cast],
        scratch_shapes=[pltpu.VMEM(w.shape, BF16) for w in matrices] + [
            pltpu.VMEM((2, STAGE_ROWS, stage_cols), F32),
            pltpu.SemaphoreType.DMA((2,)),
            pltpu.VMEM((ns, 3, POOL_TOP + tile, pw), F32),
            pltpu.VMEM((ns, tile, 4 * hw), F32),
            pltpu.VMEM((ns, tile, hw), F32),
            pltpu.VMEM((ns, n_heads, LANES, LANES), F32),
        ],
        compiler_params=pltpu.CompilerParams(
            dimension_semantics=("arbitrary", "arbitrary"), vmem_limit_bytes=VMEM_LIMIT),
        name="mixer_prompt",
    )(x, *vectors, *matrices, *to_cast)
    n_w = len(matrices)
    return outs[0], outs[1], outs[2], outs[3:3 + n_w], outs[3 + n_w:]


def _mixer_sample(x, state_pool, state_hgrn, weights, *, tile_b):
    bsz, seq, d_model = x.shape
    buf, _, pw = state_pool.shape
    _, n_heads, dk, dv = state_hgrn.shape
    hw = n_heads * dv
    assert bsz % tile_b == 0 and seq == SUBLANES and buf >= max(POOL_WINDOWS) - 1
    assert dk == LANES and dv == LANES
    rows = tile_b * seq
    xf = x.reshape(bsz * seq, d_model)
    return pl.pallas_call(
        functools.partial(_mixer_sample_kernel, seq=seq),
        grid=(bsz // tile_b,),
        in_specs=[
            pl.BlockSpec((rows, d_model), lambda i: (i, 0)),
            pl.BlockSpec((buf, tile_b, pw), lambda i: (0, i, 0)),
            pl.BlockSpec((tile_b, n_heads, dk, dv), lambda i: (i, 0, 0, 0)),
        ] + [_whole(w) for w in weights],
        out_specs=[
            pl.BlockSpec((rows, d_model), lambda i: (i, 0)),
            pl.BlockSpec((buf, tile_b, pw), lambda i: (0, i, 0)),
            pl.BlockSpec((tile_b, n_heads, dk, dv), lambda i: (i, 0, 0, 0)),
        ],
        out_shape=[
            jax.ShapeDtypeStruct((bsz * seq, d_model), F32),
            jax.ShapeDtypeStruct((buf, bsz, pw), F32),
            jax.ShapeDtypeStruct((bsz, n_heads, dk, dv), F32),
        ],
        scratch_shapes=[
            pltpu.VMEM((pw // LANES, rows, LANES), F32),
            pltpu.VMEM((rows, 4 * hw), F32),
            pltpu.VMEM((rows, hw), F32),
            pltpu.VMEM((rows, hw), F32),
            pltpu.VMEM((rows, hw), F32),
            pltpu.VMEM((rows, hw), F32),
            pltpu.VMEM((rows, hw), F32),
        ],
        compiler_params=pltpu.CompilerParams(
            dimension_semantics=("parallel",), vmem_limit_bytes=VMEM_LIMIT),
        name="mixer_sample",
    )(xf, state_pool, state_hgrn, *weights)


def _ffn_ple(x_p, x_s, p_p, p_s, weights, *, tile):
    (n_p, d_model), n_s = x_p.shape, x_s.shape[0]
    p_dim = p_p.shape[1]
    assert n_p % tile == 0 and n_s % tile == 0
    tp, ts = n_p // tile, n_s // tile

    def prompt_idx(i):
        return (jnp.minimum(i, tp - 1), 0)

    def sample_idx(i):
        return (jnp.maximum(i - tp, 0), 0)

    return pl.pallas_call(
        functools.partial(_ffn_ple_kernel, n_prompt=tp),
        grid=(tp + ts,),
        in_specs=[
            pl.BlockSpec((tile, d_model), prompt_idx),
            pl.BlockSpec((tile, d_model), sample_idx),
            pl.BlockSpec((tile, p_dim), prompt_idx),
            pl.BlockSpec((tile, p_dim), sample_idx),
        ] + [_whole(w) for w in weights],
        out_specs=[
            pl.BlockSpec((tile, d_model), prompt_idx),
            pl.BlockSpec((tile, d_model), sample_idx),
        ],
        out_shape=[
            jax.ShapeDtypeStruct((n_p, d_model), F32),
            jax.ShapeDtypeStruct((n_s, d_model), F32),
        ],
        compiler_params=pltpu.CompilerParams(
            dimension_semantics=("arbitrary",), vmem_limit_bytes=VMEM_LIMIT),
        name="ffn_ple",
    )(x_p, x_s, p_p, p_s, *weights)


def kernel(x_prompt, x_sample, state_pool, state_hgrn, p_prompt, p_sample, g_mix, w_in, w_pool_mix, pool_scale, hgrn_lb, hgrn_norm, w_pool_up, w_hgrn_up, w_out, g_ffn, w_ffn_gate, w_ffn_up, w_ffn_down, g_ple, w_ple_gate, w_ple_proj, g_final):
    depth = w_in.shape[0]
    assert depth == 1 and hgrn_lb.shape[0] == 2
    bsz, seq, d_model = x_prompt.shape
    dbsz, dseq, _ = x_sample.shape
    buf = state_pool.shape[2]

    vectors = (g_mix, pool_scale, hgrn_lb, hgrn_norm)
    matrices = (w_in[0], w_pool_mix[0].reshape(-1, w_pool_mix.shape[-1]), w_pool_up[0], w_hgrn_up[0],
                w_out[0])
    x1_p, pool_p, hgrn_p, (wi, wpm, wpu, whu, wo), (wg, wu, wd, wpg, wpp) = _mixer_prompt(
        x_prompt, vectors, matrices,
        (w_ffn_gate[0], w_ffn_up[0], w_ffn_down[0], w_ple_gate[0], w_ple_proj[0]), tile=512, chunk=128)
    mixer_w = (g_mix, wi, wpm, pool_scale, hgrn_lb, hgrn_norm, wpu, whu, wo)
    ffn_w = (g_ffn, wg, wu, wd, g_ple, wpg, wpp, g_final.reshape(1, d_model))
    x1_s, pool_s, hgrn_s = _mixer_sample(x_sample, jnp.swapaxes(state_pool[0], 0, 1), state_hgrn[0],
                                         mixer_w, tile_b=16)

    y_p, y_s = _ffn_ple(x1_p.reshape(bsz * seq, d_model), x1_s,
                        p_prompt[0].reshape(bsz * seq, -1), p_sample[0].reshape(dbsz * dseq, -1),
                        ffn_w, tile=512)

    return (y_p.reshape(bsz, seq, d_model), y_s.reshape(dbsz, dseq, d_model),
            pool_p[None, :, POOL_HALO - buf:, :], hgrn_p[None],
            jnp.swapaxes(pool_s, 0, 1)[None], hgrn_s[None])
```

```python
import functools

import jax
import jax.numpy as jnp
from jax import lax
from jax.experimental import pallas as pl
from jax.experimental.pallas import tpu as pltpu

F32 = jnp.float32
BF16 = jnp.bfloat16

EPS = 1e-6
POOL_WINDOWS = (2, 4, 8, 16)
assert POOL_WINDOWS == tuple(2 << g for g in range(len(POOL_WINDOWS)))
POOL_HALO = 16
POOL_PAD = 8
POOL_TOP = POOL_PAD + POOL_HALO
SUB = 16
LANES = 128
SUBLANES = 8
VMEM_LIMIT = 56 * 1024 * 1024
GATE_PIECE = 256
FFN_ROW_GROUPS = 2
FFN_PIECE = 768
N_MIXER_WEIGHTS = 5
STAGE_ROWS = 128
SAMPLE_UNROLL = 4
PROMPT_STREAMS = 1

_NT = (((1,), (1,)), ((), ()))
_TN = (((0,), (0,)), ((), ()))


def _bf(x):
    return x.astype(BF16)


def _dot(a, b):
    return jnp.dot(a, b, preferred_element_type=F32)


def _rms(x, gain):
    ms = jnp.mean(x * x, axis=-1, keepdims=True)
    return x * lax.rsqrt(ms + EPS) * gain


def _silu(x):
    return x * jax.nn.sigmoid(x)


def _forget_lower_bound(lb_ref):
    a0 = lb_ref[0:1, :]
    a1 = lb_ref[1:2, :]
    m = jnp.maximum(a0, a1)
    e0 = jnp.exp(a0 - m)
    e1 = jnp.exp(a1 - m)
    return e0 / (e0 + e1)


def _scan_rows8(x, pos8):
    y = x.reshape(x.shape[0] // SUBLANES, SUBLANES, x.shape[1])
    s = 1
    while s < SUBLANES:
        y = y + jnp.where(pos8 >= s, pltpu.roll(y, s, axis=1), 0.0)
        s *= 2
    return y


def _last_row(y):
    return jnp.broadcast_to(y[..., SUBLANES - 1:SUBLANES, :], y.shape)


def _finish_head(a, v, s_prev, q_big, k_big, dec_row):
    vb = _bf(v)
    if a.shape[1] % LANES == 0:
        o = _dot(jnp.concatenate([_bf(a), _bf(q_big)], axis=1),
                 jnp.concatenate([vb, _bf(s_prev)], axis=0))
    else:
        o = _dot(_bf(a), vb) + _dot(_bf(q_big), _bf(s_prev))
    ds = lax.dot_general(_bf(k_big), vb, _TN, preferred_element_type=F32)
    dk = s_prev.shape[0]
    dec = jnp.transpose(jnp.broadcast_to(dec_row, (dk, dk)))
    return o, dec * s_prev + ds


def _scale_blocks(x, factors):
    out = []
    for i, f in enumerate(factors):
        blk = x[i * SUB:(i + 1) * SUB]
        out.append(blk if f is None else blk * jnp.concatenate([f] * (SUB // SUBLANES), axis=0))
    return jnp.concatenate(out, axis=0)


def _chunk_front(q, k, logf, pos8):
    n = q.shape[0]
    nb = n // SUB
    y = _scan_rows8(logf, pos8)
    cs_blk, tots = [], []
    for i in range(nb):
        lo = y[2 * i]
        hi = y[2 * i + 1] + _last_row(y[2 * i])
        cs_blk += [lo, hi]
        tots.append(_last_row(hi))
    cs = jnp.concatenate(cs_blk, axis=0)
    rs = jnp.concatenate([t for t in tots for _ in range(SUB // SUBLANES)], axis=0) - cs
    ecs = jnp.exp(cs)
    qe = q * ecs
    ke = k * jnp.exp(-cs)
    kd = k * jnp.exp(rs)

    def span(lo, hi):
        acc = None
        for m in range(lo, hi):
            acc = tots[m] if acc is None else acc + tots[m]
        return acc

    def expo(t):
        return None if t is None else jnp.exp(t)

    levels = []
    w = 2
    while w * SUB < n:
        qf = [expo(span((i // w) * w, i)) if (i // w) % 2 == 1 else None for i in range(nb)]
        kf = [expo(span(i + 1, (i // w + 1) * w)) if (i // w) % 2 == 0 else None for i in range(nb)]
        levels.append((_bf(_scale_blocks(qe, qf)), _bf(_scale_blocks(kd, kf))))
        w *= 2
    return {
        "q01": _bf(qe), "k01": _bf(jnp.concatenate([ke, kd], axis=0)),
        "levels": levels,
        "q_big": _bf(_scale_blocks(qe, [expo(span(0, i)) for i in range(nb)])),
        "k_big": _bf(_scale_blocks(kd, [expo(span(i + 1, nb)) for i in range(nb)])),
        "dec_row": jnp.exp(span(0, nb))[0:1, :],
    }


def _chunk_back(fr, v, s_prev, masks):
    n = fr["q01"].shape[0]
    a01 = lax.dot_general(fr["q01"], fr["k01"], _NT, preferred_element_type=F32)
    a = jnp.where(masks[0], a01[:, :n], jnp.where(masks[1], a01[:, n:], 0.0))
    for lvl, (q_l, k_l) in enumerate(fr["levels"]):
        a_l = lax.dot_general(q_l, k_l, _NT, preferred_element_type=F32)
        a = jnp.where(masks[lvl + 2], a_l, a)
    return _finish_head(a, v, s_prev, fr["q_big"], fr["k_big"], fr["dec_row"])


def _gated_merge_out(x, ya, ga, gb, o_raw, zg, hgrn_norm_ref, w_hgrn_up_ref, w_out_ref, n_heads):
    pieces = []
    for hh in range(n_heads):
        ln = slice(hh * LANES, (hh + 1) * LANES)
        oh = o_raw[:, ln]
        oh = oh * lax.rsqrt(jnp.mean(oh * oh, axis=-1, keepdims=True) + EPS)
        pieces.append(oh * hgrn_norm_ref[:, ln])
    o = jnp.concatenate(pieces, axis=-1) * _silu(zg)
    yb = _dot(_bf(o), w_hgrn_up_ref[...])
    merged = jax.nn.sigmoid(ga) * ya + jax.nn.sigmoid(gb) * yb
    return x + _dot(_bf(merged), w_out_ref[...])


def _pool_project(pooled, w_pool_mix_ref, pool_scale_ref, w_pool_up_ref):
    pieces = [_dot(_bf(pg), w_pool_mix_ref[g * LANES:(g + 1) * LANES, :]) for g, pg in enumerate(pooled)]
    pool_out = jnp.concatenate(pieces, axis=-1) * pool_scale_ref[...]
    return _dot(_bf(pool_out), w_pool_up_ref[...])


def _prompt_stream(s, t, x_ref, g_mix_ref, w_in_ref, w_pool_mix_ref, pool_scale_ref, lb_ref,
                   hgrn_norm_ref, w_pool_up_ref, w_hgrn_up_ref, w_out_ref, x1_ref,
                   u_scr, z_scr, o_scr, s_scr, chunk, pos8, masks):
    _, tm, d_model = x_ref.shape
    pw = u_scr.shape[3]
    hw = o_scr.shape[2]
    n_heads = hw // LANES
    col_ga = pw + 4 * hw
    st = {"gates": [], "pooled": [None] * len(POOL_WINDOWS)}

    def norm():
        st["h"] = _bf(_rms(x_ref[s], g_mix_ref[...]))
        st["lb"] = _forget_lower_bound(lb_ref)

    def proj_u():
        u_scr[s, 0, POOL_TOP:POOL_TOP + tm, :] = _dot(st["h"], w_in_ref[:, 0:pw])

    def proj_z(c0, c1):
        def run():
            z_scr[s, :, c0:c1] = _dot(st["h"], w_in_ref[:, pw + c0:pw + c1])
        return run

    def window_level(lv):
        def run():
            shift = 1 << lv
            lanes = slice(lv * LANES, pw)
            src = 0 if lv == 0 else 1 + (lv - 1) % 2
            rows_all = POOL_HALO + tm
            sums = (u_scr[s, src, POOL_PAD:POOL_PAD + rows_all, lanes]
                    + u_scr[s, src, pl.ds(POOL_PAD - shift, rows_all), lanes])
            if lanes.start + LANES < pw:
                u_scr[s, 1 + lv % 2, POOL_PAD:POOL_PAD + rows_all, lanes] = sums
            seen = lax.broadcasted_iota(jnp.int32, (tm, 1), 0) + (t * tm + 1)
            inv_cnt = 1.0 / jnp.minimum(seen, 2 * shift).astype(F32)
            ug = u_scr[s, 0, POOL_TOP:POOL_TOP + tm, lv * LANES:(lv + 1) * LANES]
            st["pooled"][lv] = sums[POOL_HALO:, 0:LANES] * inv_cnt - ug
        return run

    def pool():
        st["ya"] = _pool_project(st["pooled"], w_pool_mix_ref, pool_scale_ref, w_pool_up_ref)
        u_scr[s, 0, POOL_PAD:POOL_TOP, :] = u_scr[s, 0, POOL_PAD + tm:POOL_TOP + tm, :]

    steps = [(c, hh) for c in range(tm // chunk) for hh in range(n_heads)]
    half = n_heads // 2 * LANES

    def front(k):
        c, hh = steps[k]

        def run():
            rows = slice(c * chunk, (c + 1) * chunk)
            zq = z_scr[s, rows, hh * LANES:(hh + 1) * LANES]
            zf = z_scr[s, rows, hw + hh * LANES:hw + (hh + 1) * LANES]
            lbh = st["lb"][:, hh * LANES:(hh + 1) * LANES]
            fg = lbh + (1.0 - lbh) * jax.nn.sigmoid(zf)
            st["front"][k] = _chunk_front(_silu(zq), 1.0 - fg, jnp.log(fg), pos8)
        return run

    def back(k):
        c, hh = steps[k]

        def run():
            rows = slice(c * chunk, (c + 1) * chunk)
            v = z_scr[s, rows, 2 * hw + hh * LANES:2 * hw + (hh + 1) * LANES]
            o, s_new = _chunk_back(st["front"][k], v, s_scr[s, hh], masks)
            s_scr[s, hh] = s_new
            o_scr[s, rows, hh * LANES:(hh + 1) * LANES] = o
        return run

    def gate(c0):
        def run():
            st["gates"].append(_dot(st["h"], w_in_ref[:, c0:c0 + GATE_PIECE]))
        return run

    def tail():
        gates = jnp.concatenate(st["gates"], axis=1)
        x1_ref[s] = _gated_merge_out(x_ref[s], st["ya"], gates[:, :d_model], gates[:, d_model:],
                                     o_scr[s], z_scr[s, :, 3 * hw:4 * hw], hgrn_norm_ref,
                                     w_hgrn_up_ref, w_out_ref, n_heads)

    st["front"] = [None] * len(steps)
    gate_cols = [(col_ga + j * GATE_PIECE) for j in range(2 * d_model // GATE_PIECE)]
    early = [k for k, (_, hh) in enumerate(steps) if hh < n_heads // 2]
    late = [k for k, (_, hh) in enumerate(steps) if hh >= n_heads // 2]
    return [
        [[norm]],
        [[proj_u, proj_z(0, half), proj_z(hw, hw + half)]],
        [[proj_z(half, hw), proj_z(hw + half, 2 * hw), proj_z(2 * hw, 3 * hw), proj_z(3 * hw, 4 * hw)],
         [front(k) for k in early + late], [window_level(lv) for lv in range(len(POOL_WINDOWS))]],
        [[back(k) for k in range(len(steps))], [gate(c0) for c0 in gate_cols] + [pool]],
        [[tail]],
    ]


def _merge_evenly(lists):
    keyed = [((i + 0.5) / len(lst), n, fn) for n, lst in enumerate(lists) for i, fn in enumerate(lst)]
    return [fn for _, _, fn in sorted(keyed, key=lambda e: e[:2])]


def _emit_staggered(streams):
    n_stages = len(streams[0])
    for slot in range(n_stages + len(streams) - 1):
        lists = []
        for s, stages in enumerate(streams):
            if 0 <= slot - s < n_stages:
                lists += stages[slot - s]
        for phase in _merge_evenly(lists):
            phase()


def _stage_weights(hbm_refs, bf_refs, stages, sem):
    jobs = []
    for src, dst in zip(hbm_refs, bf_refs):
        stage = next(b for b in stages if b.shape[2] >= src.shape[1])
        rows = min(stage.shape[1], src.shape[0])
        jobs += [(src, dst, stage, rows, k) for k in range(src.shape[0] // rows)]

    def window(j):
        src, _, stage, rows, _ = jobs[j]
        return stage.at[j % 2, pl.ds(0, rows), pl.ds(0, src.shape[1])]

    def chunk_copy(j):
        src, _, _, rows, k = jobs[j]
        return pltpu.make_async_copy(src.at[pl.ds(k * rows, rows), :], window(j), sem.at[j % 2])

    chunk_copy(0).start()
    for j, (src, dst, _, rows, k) in enumerate(jobs):
        if j + 1 < len(jobs):
            chunk_copy(j + 1).start()
        chunk_copy(j).wait()
        dst[k * rows:(k + 1) * rows, :] = _bf(window(j)[...])


def _mixer_prompt_kernel(x_ref, g_mix_ref, pool_scale_ref, lb_ref, hgrn_norm_ref, *rest,
                         chunk, n_cast):
    n_w = N_MIXER_WEIGHTS
    w_hbm = rest[:n_w]
    cast_in = rest[n_w:n_w + n_cast]
    x1_ref, pool_out_ref, hgrn_out_ref = rest[n_w + n_cast:n_w + n_cast + 3]
    w_out_slabs = rest[n_w + n_cast + 3:2 * n_w + n_cast + 3]
    cast_out = rest[2 * n_w + n_cast + 3:2 * n_w + 2 * n_cast + 3]
    w_bf = rest[2 * n_w + 2 * n_cast + 3:3 * n_w + 2 * n_cast + 3]
    stage_tall, stage_wide, sem, u_scr, z_scr, o_scr, s_scr = rest[3 * n_w + 2 * n_cast + 3:]
    w_in_ref, w_pool_mix_ref, w_pool_up_ref, w_hgrn_up_ref, w_out_ref = w_bf
    n_seq = x_ref.shape[0]
    t = pl.program_id(1)
    step = pl.program_id(0) * pl.num_programs(1) + t

    @pl.when(step == 0)
    def _():
        _stage_weights(w_hbm, w_bf, (stage_tall, stage_wide), sem)

    @pl.when(t == 0)
    def _():
        u_scr[:, :, 0:POOL_TOP, :] = jnp.zeros(u_scr.shape[:2] + (POOL_TOP, u_scr.shape[3]), F32)
        s_scr[...] = jnp.zeros(s_scr.shape, F32)

    pos8 = lax.broadcasted_iota(jnp.int32, (1, SUBLANES, LANES), 1)
    ti = lax.broadcasted_iota(jnp.int32, (chunk, chunk), 0)
    si = lax.broadcasted_iota(jnp.int32, (chunk, chunk), 1)
    masks = [((ti // SUB) == (si // SUB)) & (si <= ti)]
    b = SUB
    while b < chunk:
        masks.append(((ti // (2 * b)) == (si // (2 * b))) & ((ti & b) != 0) & ((si & b) == 0))
        b *= 2

    streams = [_prompt_stream(s, t, x_ref, g_mix_ref, w_in_ref, w_pool_mix_ref, pool_scale_ref, lb_ref,
                              hgrn_norm_ref, w_pool_up_ref, w_hgrn_up_ref, w_out_ref, x1_ref,
                              u_scr, z_scr, o_scr, s_scr, chunk, pos8, masks) for s in range(n_seq)]

    def cast_slabs():
        for src, dst in zip(cast_in, cast_out):
            dst[...] = _bf(src[...])
        for src, dst in zip(w_bf, w_out_slabs):
            rows = dst.shape[0]
            dst[...] = src[pl.ds(pl.multiple_of(step * rows, rows), rows), :]

    streams[0][1].append([cast_slabs])
    _emit_staggered(streams)

    @pl.when(t == pl.num_programs(1) - 1)
    def _():
        pool_out_ref[...] = u_scr[:, 0, POOL_PAD:POOL_TOP, :]
        hgrn_out_ref[...] = s_scr[...]


def _mixer_sample_kernel(x_ref, sp_ref, sh_ref, g_mix_ref, w_in_ref, w_pool_mix_ref, pool_scale_ref,
                         lb_ref, hgrn_norm_ref, w_pool_up_ref, w_hgrn_up_ref, w_out_ref,
                         x1_ref, pool_out_ref, hgrn_out_ref,
                         e_scr, z_scr, o_scr, qe_scr, ke_scr, kd_scr, dec_scr, *, seq):
    rows_n, d_model = x_ref.shape
    buf, tb, pw = sp_ref.shape
    hw = o_scr.shape[1]
    n_heads = hw // LANES

    x = x_ref[...]
    h = _bf(_rms(x, g_mix_ref[...]))

    u = _dot(h, w_in_ref[:, 0:pw])
    pooled = []
    for g, w in enumerate(POOL_WINDOWS):
        ln = slice(g * LANES, (g + 1) * LANES)
        e_scr[g] = u[:, ln]
        ext = [sp_ref[e, :, ln] for e in range(buf)]
        ext += [e_scr[g, pl.ds(tt, tb, stride=seq), :] for tt in range(seq)]
        for e in range(buf):
            pool_out_ref[e, :, ln] = ext[seq + e]
        for tt in range(seq):
            acc = ext[buf + tt]
            for j in range(1, w):
                acc = acc + ext[buf + tt - j]
            e_scr[g, pl.ds(tt, tb, stride=seq), :] = acc * (1.0 / w) - ext[buf + tt]
        pooled.append(e_scr[g])
    col_ga = pw + 4 * hw
    ya = _pool_project(pooled, w_pool_mix_ref, pool_scale_ref, w_pool_up_ref)

    z_scr[...] = _dot(h, w_in_ref[:, pw:pw + 4 * hw])
    lb = _forget_lower_bound(lb_ref)
    pos8 = lax.broadcasted_iota(jnp.int32, (1, SUBLANES, LANES), 1)
    for hh in range(n_heads):
        ln = slice(hh * LANES, (hh + 1) * LANES)
        zq = z_scr[:, hh * LANES:(hh + 1) * LANES]
        zf = z_scr[:, hw + hh * LANES:hw + (hh + 1) * LANES]
        lbh = lb[:, ln]
        fg = lbh + (1.0 - lbh) * jax.nn.sigmoid(zf)
        k = 1.0 - fg
        q = _silu(zq)
        y = _scan_rows8(jnp.log(fg), pos8)
        tot = _last_row(y)
        cs = y.reshape(rows_n, LANES)
        qe_scr[:, ln] = q * jnp.exp(cs)
        ke_scr[:, ln] = k * jnp.exp(-cs)
        kd_scr[:, ln] = k * jnp.exp((tot - y).reshape(rows_n, LANES))
        dec_scr[:, ln] = jnp.exp(tot).reshape(rows_n, LANES)

    ti = lax.broadcasted_iota(jnp.int32, (seq, seq), 0)
    si = lax.broadcasted_iota(jnp.int32, (seq, seq), 1)
    causal = si <= ti

    def seq_body(b, carry):
        r0 = pl.multiple_of(b * seq, seq)
        rows = pl.ds(r0, seq)
        for hh in range(n_heads):
            ln = slice(hh * LANES, (hh + 1) * LANES)
            qe = qe_scr[rows, ln]
            v = z_scr[rows, 2 * hw + hh * LANES:2 * hw + (hh + 1) * LANES]
            a = lax.dot_general(_bf(qe), _bf(ke_scr[rows, ln]), _NT, preferred_element_type=F32)
            o, s_new = _finish_head(jnp.where(causal, a, 0.0), v, sh_ref[b, hh], qe,
                                    kd_scr[rows, ln], dec_scr[pl.ds(r0, 1), ln])
            hgrn_out_ref[b, hh] = s_new
            o_scr[rows, ln] = o
        return carry

    lax.fori_loop(0, tb, seq_body, 0, unroll=SAMPLE_UNROLL)

    gates = _dot(h, w_in_ref[:, col_ga:col_ga + 2 * d_model])
    x1_ref[...] = _gated_merge_out(x, ya, gates[:, :d_model], gates[:, d_model:], o_scr[...],
                                   z_scr[:, 3 * hw:4 * hw], hgrn_norm_ref, w_hgrn_up_ref, w_out_ref,
                                   n_heads)


def _ffn_ple_kernel(xp_ref, xs_ref, pp_ref, ps_ref, g_ffn_ref, w_gate_ref, w_up_ref, w_down_ref,
                    g_ple_ref, w_ple_gate_ref, w_ple_proj_ref, g_final_ref, yp_ref, ys_ref, *, n_prompt):
    d_ff = w_gate_ref.shape[1]
    ff_cuts = list(range(0, d_ff, FFN_PIECE)) + [d_ff]

    def row_group(x_ref, p_ref, y_ref, rows):
        st = {"act": []}

        def norm():
            st["x"] = x_ref[rows, :]
            st["h2"] = _bf(_rms(st["x"], g_ffn_ref[...]))

        def ff(c0, c1):
            def run():
                g = _dot(st["h2"], w_gate_ref[:, c0:c1])
                st["act"].append(_bf(_silu(g) * _dot(st["h2"], w_up_ref[:, c0:c1])))
            return run

        def down():
            x = st["x"] + _dot(jnp.concatenate(st["act"], axis=1), w_down_ref[...])
            st["x"] = x
            st["h3"] = _bf(_rms(x, g_ple_ref[...]))

        def ple():
            gate = jax.nn.sigmoid(_dot(st["h3"], w_ple_gate_ref[...]))
            emb = _dot(_bf(p_ref[rows, :]), w_ple_proj_ref[...])
            y_ref[rows, :] = _rms(st["x"] + gate * emb, g_final_ref[...])

        return [[[norm]], [[ff(c0, c1) for c0, c1 in zip(ff_cuts, ff_cuts[1:])]], [[down]], [[ple]]]

    def tile(x_ref, p_ref, y_ref):
        n = x_ref.shape[0] // FFN_ROW_GROUPS
        _emit_staggered([row_group(x_ref, p_ref, y_ref, slice(r * n, (r + 1) * n))
                         for r in range(FFN_ROW_GROUPS)])

    i = pl.program_id(0)

    @pl.when(i < n_prompt)
    def _():
        tile(xp_ref, pp_ref, yp_ref)

    @pl.when(i >= n_prompt)
    def _():
        tile(xs_ref, ps_ref, ys_ref)


def _whole(_):
    return pl.BlockSpec(memory_space=pltpu.VMEM)


def _mixer_prompt(x, vectors, matrices, to_cast, *, tile, chunk):
    bsz, seq, d_model = x.shape
    pw = vectors[1].shape[1]
    hw = vectors[3].shape[1]
    n_heads = hw // LANES
    ns = PROMPT_STREAMS
    assert len(matrices) == N_MIXER_WEIGHTS
    assert seq % tile == 0 and tile % chunk == 0 and chunk % SUB == 0 and bsz % ns == 0
    nt = seq // tile
    n_steps = (bsz // ns) * nt

    def row_slab(w, exact):
        rep = 1 if w.shape[0] % (n_steps * 2 * SUBLANES) == 0 else 2
        rows = w.shape[0] * rep // n_steps
        assert rows * n_steps == w.shape[0] * rep and rows % (2 * SUBLANES) == 0 and (rep == 1 or not exact)
        return pl.BlockSpec((rows, w.shape[1]), lambda b, t, rep=rep: ((b * nt + t) // rep, 0))

    slabs = [row_slab(w, False) for w in to_cast]
    w_slabs = [row_slab(w, True) for w in matrices]
    wide_cols = max(w.shape[1] for w in matrices)
    tall_rows = STAGE_ROWS * wide_cols // d_model // STAGE_ROWS * STAGE_ROWS
    assert all(w.shape[0] % STAGE_ROWS == 0 and (w.shape[1] > d_model or w.shape[0] % tall_rows == 0)
               for w in matrices)
    outs = pl.pallas_call(
        functools.partial(_mixer_prompt_kernel, chunk=chunk, n_cast=len(to_cast)),
        grid=(bsz // ns, nt),
        in_specs=([pl.BlockSpec((ns, tile, d_model), lambda b, t: (b, t, 0))]
                  + [_whole(v) for v in vectors]
                  + [pl.BlockSpec(memory_space=pl.ANY) for _ in matrices] + slabs),
        out_specs=[
            pl.BlockSpec((ns, tile, d_model), lambda b, t: (b, t, 0)),
            pl.BlockSpec((ns, POOL_HALO, pw), lambda b, t: (b, 0, 0)),
            pl.BlockSpec((ns, n_heads, LANES, LANES), lambda b, t: (b, 0, 0, 0)),
        ] + w_slabs + slabs,
        out_shape=[
            jax.ShapeDtypeStruct((bsz, seq, d_model), F32),
            jax.ShapeDtypeStruct((bsz, POOL_HALO, pw), F32),
            jax.ShapeDtypeStruct((bsz, n_heads, LANES, LANES), F32),
        ] + [jax.ShapeDtypeStruct(w.shape, BF16) for w in matrices + to_cast],
        scratch_shapes=[pltpu.VMEM(w.shape, BF16) for w in matrices] + [
            pltpu.VMEM((2, tall_rows, d_model), F32),
            pltpu.VMEM((2, STAGE_ROWS, wide_cols), F32),
            pltpu.SemaphoreType.DMA((2,)),
            pltpu.VMEM((ns, 3, POOL_TOP + tile, pw), F32),
            pltpu.VMEM((ns, tile, 4 * hw), F32),
            pltpu.VMEM((ns, tile, hw), F32),
            pltpu.VMEM((ns, n_heads, LANES, LANES), F32),
        ],
        compiler_params=pltpu.CompilerParams(
            dimension_semantics=("arbitrary", "arbitrary"), vmem_limit_bytes=VMEM_LIMIT),
        name="mixer_prompt",
    )(x, *vectors, *matrices, *to_cast)
    n_w = len(matrices)
    return outs[0], outs[1], outs[2], outs[3:3 + n_w], outs[3 + n_w:]


def _mixer_sample(x, state_pool, state_hgrn, weights, *, tile_b):
    bsz, seq, d_model = x.shape
    buf, _, pw = state_pool.shape
    _, n_heads, dk, dv = state_hgrn.shape
    hw = n_heads * dv
    assert bsz % tile_b == 0 and seq == SUBLANES and buf >= max(POOL_WINDOWS) - 1
    assert dk == LANES and dv == LANES
    rows = tile_b * seq
    xf = x.reshape(bsz * seq, d_model)
    return pl.pallas_call(
        functools.partial(_mixer_sample_kernel, seq=seq),
        grid=(bsz // tile_b,),
        in_specs=[
            pl.BlockSpec((rows, d_model), lambda i: (i, 0)),
            pl.BlockSpec((buf, tile_b, pw), lambda i: (0, i, 0)),
            pl.BlockSpec((tile_b, n_heads, dk, dv), lambda i: (i, 0, 0, 0)),
        ] + [_whole(w) for w in weights],
        out_specs=[
            pl.BlockSpec((rows, d_model), lambda i: (i, 0)),
            pl.BlockSpec((buf, tile_b, pw), lambda i: (0, i, 0)),
            pl.BlockSpec((tile_b, n_heads, dk, dv), lambda i: (i, 0, 0, 0)),
        ],
        out_shape=[
            jax.ShapeDtypeStruct((bsz * seq, d_model), F32),
            jax.ShapeDtypeStruct((buf, bsz, pw), F32),
            jax.ShapeDtypeStruct((bsz, n_heads, dk, dv), F32),
        ],
        scratch_shapes=[
            pltpu.VMEM((pw // LANES, rows, LANES), F32),
            pltpu.VMEM((rows, 4 * hw), F32),
            pltpu.VMEM((rows, hw), F32),
            pltpu.VMEM((rows, hw), F32),
            pltpu.VMEM((rows, hw), F32),
            pltpu.VMEM((rows, hw), F32),
            pltpu.VMEM((rows, hw), F32),
        ],
        compiler_params=pltpu.CompilerParams(
            dimension_semantics=("parallel",), vmem_limit_bytes=VMEM_LIMIT),
        name="mixer_sample",
    )(xf, state_pool, state_hgrn, *weights)


def _ffn_ple(x_p, x_s, p_p, p_s, weights, *, tile):
    (n_p, d_model), n_s = x_p.shape, x_s.shape[0]
    p_dim = p_p.shape[1]
    assert n_p % tile == 0 and n_s % tile == 0
    tp, ts = n_p // tile, n_s // tile

    def prompt_idx(i):
        return (jnp.minimum(i, tp - 1), 0)

    def sample_idx(i):
        return (jnp.maximum(i - tp, 0), 0)

    return pl.pallas_call(
        functools.partial(_ffn_ple_kernel, n_prompt=tp),
        grid=(tp + ts,),
        in_specs=[
            pl.BlockSpec((tile, d_model), prompt_idx),
            pl.BlockSpec((tile, d_model), sample_idx),
            pl.BlockSpec((tile, p_dim), prompt_idx),
            pl.BlockSpec((tile, p_dim), sample_idx),
        ] + [_whole(w) for w in weights],
        out_specs=[
            pl.BlockSpec((tile, d_model), prompt_idx),
            pl.BlockSpec((tile, d_model), sample_idx),
        ],
        out_shape=[
            jax.ShapeDtypeStruct((n_p, d_model), F32),
            jax.ShapeDtypeStruct((n_s, d_model), F32),
        ],
        compiler_params=pltpu.CompilerParams(
            dimension_semantics=("arbitrary",), vmem_limit_bytes=VMEM_LIMIT),
        name="ffn_ple",
    )(x_p, x_s, p_p, p_s, *weights)


def kernel(x_prompt, x_sample, state_pool, state_hgrn, p_prompt, p_sample, g_mix, w_in, w_pool_mix, pool_scale, hgrn_lb, hgrn_norm, w_pool_up, w_hgrn_up, w_out, g_ffn, w_ffn_gate, w_ffn_up, w_ffn_down, g_ple, w_ple_gate, w_ple_proj, g_final):
    depth = w_in.shape[0]
    assert depth == 1 and hgrn_lb.shape[0] == 2
    bsz, seq, d_model = x_prompt.shape
    dbsz, dseq, _ = x_sample.shape
    buf = state_pool.shape[2]

    vectors = (g_mix, pool_scale, hgrn_lb, hgrn_norm)
    matrices = (w_in[0], w_pool_mix[0].reshape(-1, w_pool_mix.shape[-1]), w_pool_up[0], w_hgrn_up[0],
                w_out[0])
    x1_p, pool_p, hgrn_p, (wi, wpm, wpu, whu, wo), (wg, wu, wd, wpg, wpp) = _mixer_prompt(
        x_prompt, vectors, matrices,
        (w_ffn_gate[0], w_ffn_up[0], w_ffn_down[0], w_ple_gate[0], w_ple_proj[0]), tile=512, chunk=128)
    mixer_w = (g_mix, wi, wpm, pool_scale, hgrn_lb, hgrn_norm, wpu, whu, wo)
    ffn_w = (g_ffn, wg, wu, wd, g_ple, wpg, wpp, g_final.reshape(1, d_model))
    x1_s, pool_s, hgrn_s = _mixer_sample(x_sample, jnp.swapaxes(state_pool[0], 0, 1), state_hgrn[0],
                                         mixer_w, tile_b=16)

    y_p, y_s = _ffn_ple(x1_p.reshape(bsz * seq, d_model), x1_s,
                        p_prompt[0].reshape(bsz * seq, -1), p_sample[0].reshape(dbsz * dseq, -1),
                        ffn_w, tile=512)

    return (y_p.reshape(bsz, seq, d_model), y_s.reshape(dbsz, dseq, d_model),
            pool_p[None, :, POOL_HALO - buf:, :], hgrn_p[None],
            jnp.swapaxes(pool_s, 0, 1)[None], hgrn_s[None])
```

```python
import functools

import jax
import jax.numpy as jnp
from jax import lax
from jax.experimental import pallas as pl
from jax.experimental.pallas import tpu as pltpu

F32 = jnp.float32
BF16 = jnp.bfloat16

EPS = 1e-6
POOL_WINDOWS = (2, 4, 8, 16)
assert POOL_WINDOWS == tuple(2 << g for g in range(len(POOL_WINDOWS)))
POOL_HALO = 16
POOL_PAD = 8
POOL_TOP = POOL_PAD + POOL_HALO
SUB = 16
LANES = 128
SUBLANES = 8
VMEM_LIMIT = 56 * 1024 * 1024
TOKEN_TILE = 512
RECURRENCE_CHUNK = 128
SAMPLE_TILE_SEQS = 16
GATE_PIECE = 256
FFN_ROW_CUTS = (0, 1, 2)
FFN_PIECE = 768
N_MIXER_WEIGHTS = 5
STAGE_ROWS = 128
SAMPLE_UNROLL = 4
PROMPT_STREAMS = 1

_NT = (((1,), (1,)), ((), ()))
_TN = (((0,), (0,)), ((), ()))


def _bf(x):
    return x.astype(BF16)


def _dot(a, b):
    return jnp.dot(a, b, preferred_element_type=F32)


def _rms(x, gain):
    ms = jnp.mean(x * x, axis=-1, keepdims=True)
    return x * lax.rsqrt(ms + EPS) * gain


def _silu(x):
    return x * jax.nn.sigmoid(x)


def _forget_lower_bound(lb_ref):
    a0 = lb_ref[0:1, :]
    a1 = lb_ref[1:2, :]
    m = jnp.maximum(a0, a1)
    e0 = jnp.exp(a0 - m)
    e1 = jnp.exp(a1 - m)
    return e0 / (e0 + e1)


def _scan_rows8(x, pos8):
    y = x.reshape(x.shape[0] // SUBLANES, SUBLANES, x.shape[1])
    s = 1
    while s < SUBLANES:
        y = y + jnp.where(pos8 >= s, pltpu.roll(y, s, axis=1), 0.0)
        s *= 2
    return y


def _last_row(y):
    return jnp.broadcast_to(y[..., SUBLANES - 1:SUBLANES, :], y.shape)


def _finish_head(a, v, s_prev, q_big, k_big, dec_row):
    vb = _bf(v)
    if a.shape[1] % LANES == 0:
        o = _dot(jnp.concatenate([_bf(a), _bf(q_big)], axis=1),
                 jnp.concatenate([vb, _bf(s_prev)], axis=0))
    else:
        o = _dot(_bf(a), vb) + _dot(_bf(q_big), _bf(s_prev))
    ds = lax.dot_general(_bf(k_big), vb, _TN, preferred_element_type=F32)
    dk = s_prev.shape[0]
    dec = jnp.transpose(jnp.broadcast_to(dec_row, (dk, dk)))
    return o, dec * s_prev + ds


def _scale_blocks(x, factors):
    out = []
    for i, f in enumerate(factors):
        blk = x[i * SUB:(i + 1) * SUB]
        out.append(blk if f is None else blk * jnp.concatenate([f] * (SUB // SUBLANES), axis=0))
    return jnp.concatenate(out, axis=0)


def _chunk_front(q, k, logf, pos8):
    n = q.shape[0]
    nb = n // SUB
    y = _scan_rows8(logf, pos8)
    cs_blk, tots = [], []
    for i in range(nb):
        lo = y[2 * i]
        hi = y[2 * i + 1] + _last_row(y[2 * i])
        cs_blk += [lo, hi]
        tots.append(_last_row(hi))
    cs = jnp.concatenate(cs_blk, axis=0)
    rs = jnp.concatenate([t for t in tots for _ in range(SUB // SUBLANES)], axis=0) - cs
    ecs = jnp.exp(cs)
    qe = q * ecs
    ke = k * jnp.exp(-cs)
    kd = k * jnp.exp(rs)

    def span(lo, hi):
        acc = None
        for m in range(lo, hi):
            acc = tots[m] if acc is None else acc + tots[m]
        return acc

    def expo(t):
        return None if t is None else jnp.exp(t)

    levels = []
    w = 2
    while w * SUB < n:
        qf = [expo(span((i // w) * w, i)) if (i // w) % 2 == 1 else None for i in range(nb)]
        kf = [expo(span(i + 1, (i // w + 1) * w)) if (i // w) % 2 == 0 else None for i in range(nb)]
        levels.append((_bf(_scale_blocks(qe, qf)), _bf(_scale_blocks(kd, kf))))
        w *= 2
    return {
        "q01": _bf(qe), "k01": _bf(jnp.concatenate([ke, kd], axis=0)),
        "levels": levels,
        "q_big": _bf(_scale_blocks(qe, [expo(span(0, i)) for i in range(nb)])),
        "k_big": _bf(_scale_blocks(kd, [expo(span(i + 1, nb)) for i in range(nb)])),
        "dec_row": jnp.exp(span(0, nb))[0:1, :],
    }


def _chunk_back(fr, v, s_prev, masks):
    n = fr["q01"].shape[0]
    a01 = lax.dot_general(fr["q01"], fr["k01"], _NT, preferred_element_type=F32)
    a = jnp.where(masks[0], a01[:, :n], jnp.where(masks[1], a01[:, n:], 0.0))
    for lvl, (q_l, k_l) in enumerate(fr["levels"]):
        a_l = lax.dot_general(q_l, k_l, _NT, preferred_element_type=F32)
        a = jnp.where(masks[lvl + 2], a_l, a)
    return _finish_head(a, v, s_prev, fr["q_big"], fr["k_big"], fr["dec_row"])


def _gated_merge_out(x, ya, ga, gb, o_raw, zg, hgrn_norm_ref, w_hgrn_up_ref, w_out_ref, n_heads):
    pieces = []
    for hh in range(n_heads):
        ln = slice(hh * LANES, (hh + 1) * LANES)
        oh = o_raw[:, ln]
        oh = oh * lax.rsqrt(jnp.mean(oh * oh, axis=-1, keepdims=True) + EPS)
        pieces.append(oh * hgrn_norm_ref[:, ln])
    o = jnp.concatenate(pieces, axis=-1) * _silu(zg)
    yb = _dot(_bf(o), w_hgrn_up_ref[...])
    merged = jax.nn.sigmoid(ga) * ya + jax.nn.sigmoid(gb) * yb
    return x + _dot(_bf(merged), w_out_ref[...])


def _pool_project(pooled, w_pool_mix_ref, pool_scale_ref, w_pool_up_ref):
    pieces = [_dot(_bf(pg), w_pool_mix_ref[g * LANES:(g + 1) * LANES, :]) for g, pg in enumerate(pooled)]
    pool_out = jnp.concatenate(pieces, axis=-1) * pool_scale_ref[...]
    return _dot(_bf(pool_out), w_pool_up_ref[...])


def _prompt_stream(s, t, x_ref, g_mix_ref, w_in_ref, w_pool_mix_ref, pool_scale_ref, lb_ref,
                   hgrn_norm_ref, w_pool_up_ref, w_hgrn_up_ref, w_out_ref, x1_ref,
                   u_scr, z_scr, o_scr, s_scr, chunk, pos8, masks):
    _, tm, d_model = x_ref.shape
    pw = u_scr.shape[3]
    hw = o_scr.shape[2]
    n_heads = hw // LANES
    col_ga = pw + 4 * hw
    st = {"gates": [], "pooled": [None] * len(POOL_WINDOWS)}

    def norm():
        st["h"] = _bf(_rms(x_ref[s], g_mix_ref[...]))
        st["lb"] = _forget_lower_bound(lb_ref)

    def proj_u():
        u_scr[s, 0, POOL_TOP:POOL_TOP + tm, :] = _dot(st["h"], w_in_ref[:, 0:pw])

    def proj_z(c0, c1):
        def run():
            z_scr[s, :, c0:c1] = _dot(st["h"], w_in_ref[:, pw + c0:pw + c1])
        return run

    def window_level(lv):
        def run():
            shift = 1 << lv
            lanes = slice(lv * LANES, pw)
            src = 0 if lv == 0 else 1 + (lv - 1) % 2
            rows_all = POOL_HALO + tm
            sums = (u_scr[s, src, POOL_PAD:POOL_PAD + rows_all, lanes]
                    + u_scr[s, src, pl.ds(POOL_PAD - shift, rows_all), lanes])
            if lanes.start + LANES < pw:
                u_scr[s, 1 + lv % 2, POOL_PAD:POOL_PAD + rows_all, lanes] = sums
            seen = lax.broadcasted_iota(jnp.int32, (tm, 1), 0) + (t * tm + 1)
            inv_cnt = 1.0 / jnp.minimum(seen, 2 * shift).astype(F32)
            ug = u_scr[s, 0, POOL_TOP:POOL_TOP + tm, lv * LANES:(lv + 1) * LANES]
            st["pooled"][lv] = sums[POOL_HALO:, 0:LANES] * inv_cnt - ug
        return run

    def pool():
        st["ya"] = _pool_project(st["pooled"], w_pool_mix_ref, pool_scale_ref, w_pool_up_ref)
        u_scr[s, 0, POOL_PAD:POOL_TOP, :] = u_scr[s, 0, POOL_PAD + tm:POOL_TOP + tm, :]

    steps = [(c, hh) for c in range(tm // chunk) for hh in range(n_heads)]
    half = n_heads // 2 * LANES

    def front(k):
        c, hh = steps[k]

        def run():
            rows = slice(c * chunk, (c + 1) * chunk)
            zq = z_scr[s, rows, hh * LANES:(hh + 1) * LANES]
            zf = z_scr[s, rows, hw + hh * LANES:hw + (hh + 1) * LANES]
            lbh = st["lb"][:, hh * LANES:(hh + 1) * LANES]
            fg = lbh + (1.0 - lbh) * jax.nn.sigmoid(zf)
            st["front"][k] = _chunk_front(_silu(zq), 1.0 - fg, jnp.log(fg), pos8)
        return run

    def back(k):
        c, hh = steps[k]

        def run():
            rows = slice(c * chunk, (c + 1) * chunk)
            v = z_scr[s, rows, 2 * hw + hh * LANES:2 * hw + (hh + 1) * LANES]
            o, s_new = _chunk_back(st["front"][k], v, s_scr[s, hh], masks)
            s_scr[s, hh] = s_new
            o_scr[s, rows, hh * LANES:(hh + 1) * LANES] = o
        return run

    def gate(c0):
        def run():
            st["gates"].append(_dot(st["h"], w_in_ref[:, c0:c0 + GATE_PIECE]))
        return run

    def tail():
        gates = jnp.concatenate(st["gates"], axis=1)
        x1_ref[s] = _gated_merge_out(x_ref[s], st["ya"], gates[:, :d_model], gates[:, d_model:],
                                     o_scr[s], z_scr[s, :, 3 * hw:4 * hw], hgrn_norm_ref,
                                     w_hgrn_up_ref, w_out_ref, n_heads)

    st["front"] = [None] * len(steps)
    gate_cols = [(col_ga + j * GATE_PIECE) for j in range(2 * d_model // GATE_PIECE)]
    early = [k for k, (_, hh) in enumerate(steps) if hh < n_heads // 2]
    late = [k for k, (_, hh) in enumerate(steps) if hh >= n_heads // 2]
    return [
        [[norm]],
        [[proj_u, proj_z(0, half), proj_z(hw, hw + half)]],
        [[proj_z(half, hw), proj_z(hw + half, 2 * hw), proj_z(2 * hw, 3 * hw), proj_z(3 * hw, 4 * hw)],
         [front(k) for k in early + late], [window_level(lv) for lv in range(len(POOL_WINDOWS))]],
        [[back(k) for k in range(len(steps))], [gate(c0) for c0 in gate_cols] + [pool]],
        [[tail]],
    ]


def _merge_evenly(lists):
    keyed = [((i + 0.5) / len(lst), n, fn) for n, lst in enumerate(lists) for i, fn in enumerate(lst)]
    return [fn for _, _, fn in sorted(keyed, key=lambda e: e[:2])]


def _emit_staggered(streams):
    n_stages = len(streams[0])
    for slot in range(n_stages + len(streams) - 1):
        lists = []
        for s, stages in enumerate(streams):
            if 0 <= slot - s < n_stages:
                lists += stages[slot - s]
        for phase in _merge_evenly(lists):
            phase()


def _stage_weights(hbm_refs, bf_refs, stages, sem):
    jobs = []
    for src, dst in zip(hbm_refs, bf_refs):
        stage = next(b for b in stages if b.shape[2] >= src.shape[1])
        rows = min(stage.shape[1], src.shape[0])
        jobs += [(src, dst, stage, rows, k) for k in range(src.shape[0] // rows)]

    def window(j):
        src, _, stage, rows, _ = jobs[j]
        return stage.at[j % 2, pl.ds(0, rows), pl.ds(0, src.shape[1])]

    def chunk_copy(j):
        src, _, _, rows, k = jobs[j]
        return pltpu.make_async_copy(src.at[pl.ds(k * rows, rows), :], window(j), sem.at[j % 2])

    chunk_copy(0).start()
    for j, (src, dst, _, rows, k) in enumerate(jobs):
        if j + 1 < len(jobs):
            chunk_copy(j + 1).start()
        chunk_copy(j).wait()
        dst[k * rows:(k + 1) * rows, :] = _bf(window(j)[...])


def _mixer_prompt_kernel(x_ref, g_mix_ref, pool_scale_ref, lb_ref, hgrn_norm_ref, *rest,
                         chunk, n_cast):
    n_w = N_MIXER_WEIGHTS
    w_hbm = rest[:n_w]
    cast_in = rest[n_w:n_w + n_cast]
    x1_ref, pool_out_ref, hgrn_out_ref = rest[n_w + n_cast:n_w + n_cast + 3]
    w_out_slabs = rest[n_w + n_cast + 3:2 * n_w + n_cast + 3]
    cast_out = rest[2 * n_w + n_cast + 3:2 * n_w + 2 * n_cast + 3]
    w_bf = rest[2 * n_w + 2 * n_cast + 3:3 * n_w + 2 * n_cast + 3]
    stage_tall, stage_wide, sem, u_scr, z_scr, o_scr, s_scr = rest[3 * n_w + 2 * n_cast + 3:]
    w_in_ref, w_pool_mix_ref, w_pool_up_ref, w_hgrn_up_ref, w_out_ref = w_bf
    n_seq = x_ref.shape[0]
    t = pl.program_id(1)
    step = pl.program_id(0) * pl.num_programs(1) + t

    @pl.when(step == 0)
    def _():
        _stage_weights(w_hbm, w_bf, (stage_tall, stage_wide), sem)

    @pl.when(t == 0)
    def _():
        u_scr[:, :, 0:POOL_TOP, :] = jnp.zeros(u_scr.shape[:2] + (POOL_TOP, u_scr.shape[3]), F32)
        s_scr[...] = jnp.zeros(s_scr.shape, F32)

    pos8 = lax.broadcasted_iota(jnp.int32, (1, SUBLANES, LANES), 1)
    ti = lax.broadcasted_iota(jnp.int32, (chunk, chunk), 0)
    si = lax.broadcasted_iota(jnp.int32, (chunk, chunk), 1)
    masks = [((ti // SUB) == (si // SUB)) & (si <= ti)]
    b = SUB
    while b < chunk:
        masks.append(((ti // (2 * b)) == (si // (2 * b))) & ((ti & b) != 0) & ((si & b) == 0))
        b *= 2

    streams = [_prompt_stream(s, t, x_ref, g_mix_ref, w_in_ref, w_pool_mix_ref, pool_scale_ref, lb_ref,
                              hgrn_norm_ref, w_pool_up_ref, w_hgrn_up_ref, w_out_ref, x1_ref,
                              u_scr, z_scr, o_scr, s_scr, chunk, pos8, masks) for s in range(n_seq)]

    def cast_slabs():
        for src, dst in zip(cast_in, cast_out):
            dst[...] = _bf(src[...])
        for src, dst in zip(w_bf, w_out_slabs):
            rows = dst.shape[0]
            dst[...] = src[pl.ds(pl.multiple_of(step * rows, rows), rows), :]

    streams[0][1].append([cast_slabs])
    _emit_staggered(streams)

    @pl.when(t == pl.num_programs(1) - 1)
    def _():
        pool_out_ref[...] = u_scr[:, 0, POOL_PAD:POOL_TOP, :]
        hgrn_out_ref[...] = s_scr[...]


def _mixer_sample_kernel(x_ref, sp_ref, sh_ref, g_mix_ref, w_in_ref, w_pool_mix_ref, pool_scale_ref,
                         lb_ref, hgrn_norm_ref, w_pool_up_ref, w_hgrn_up_ref, w_out_ref,
                         x1_ref, pool_out_ref, hgrn_out_ref,
                         e_scr, z_scr, o_scr, qe_scr, ke_scr, kd_scr, dec_scr, *, seq):
    rows_n, d_model = x_ref.shape
    buf, tb, pw = sp_ref.shape
    hw = o_scr.shape[1]
    n_heads = hw // LANES

    x = x_ref[...]
    h = _bf(_rms(x, g_mix_ref[...]))

    u = _dot(h, w_in_ref[:, 0:pw])
    pooled = []
    for g, w in enumerate(POOL_WINDOWS):
        ln = slice(g * LANES, (g + 1) * LANES)
        e_scr[g] = u[:, ln]
        ext = [sp_ref[e, :, ln] for e in range(buf)]
        ext += [e_scr[g, pl.ds(tt, tb, stride=seq), :] for tt in range(seq)]
        for e in range(buf):
            pool_out_ref[e, :, ln] = ext[seq + e]
        for tt in range(seq):
            acc = ext[buf + tt]
            for j in range(1, w):
                acc = acc + ext[buf + tt - j]
            e_scr[g, pl.ds(tt, tb, stride=seq), :] = acc * (1.0 / w) - ext[buf + tt]
        pooled.append(e_scr[g])
    col_ga = pw + 4 * hw
    ya = _pool_project(pooled, w_pool_mix_ref, pool_scale_ref, w_pool_up_ref)

    z_scr[...] = _dot(h, w_in_ref[:, pw:pw + 4 * hw])
    lb = _forget_lower_bound(lb_ref)
    pos8 = lax.broadcasted_iota(jnp.int32, (1, SUBLANES, LANES), 1)
    for hh in range(n_heads):
        ln = slice(hh * LANES, (hh + 1) * LANES)
        zq = z_scr[:, hh * LANES:(hh + 1) * LANES]
        zf = z_scr[:, hw + hh * LANES:hw + (hh + 1) * LANES]
        lbh = lb[:, ln]
        fg = lbh + (1.0 - lbh) * jax.nn.sigmoid(zf)
        k = 1.0 - fg
        q = _silu(zq)
        y = _scan_rows8(jnp.log(fg), pos8)
        tot = _last_row(y)
        cs = y.reshape(rows_n, LANES)
        qe_scr[:, ln] = q * jnp.exp(cs)
        ke_scr[:, ln] = k * jnp.exp(-cs)
        kd_scr[:, ln] = k * jnp.exp((tot - y).reshape(rows_n, LANES))
        dec_scr[:, ln] = jnp.exp(tot).reshape(rows_n, LANES)

    ti = lax.broadcasted_iota(jnp.int32, (seq, seq), 0)
    si = lax.broadcasted_iota(jnp.int32, (seq, seq), 1)
    causal = si <= ti

    def seq_body(b, carry):
        r0 = pl.multiple_of(b * seq, seq)
        rows = pl.ds(r0, seq)
        for hh in range(n_heads):
            ln = slice(hh * LANES, (hh + 1) * LANES)
            qe = qe_scr[rows, ln]
            v = z_scr[rows, 2 * hw + hh * LANES:2 * hw + (hh + 1) * LANES]
            a = lax.dot_general(_bf(qe), _bf(ke_scr[rows, ln]), _NT, preferred_element_type=F32)
            o, s_new = _finish_head(jnp.where(causal, a, 0.0), v, sh_ref[b, hh], qe,
                                    kd_scr[rows, ln], dec_scr[pl.ds(r0, 1), ln])
            hgrn_out_ref[b, hh] = s_new
            o_scr[rows, ln] = o
        return carry

    lax.fori_loop(0, tb, seq_body, 0, unroll=SAMPLE_UNROLL)

    gates = _dot(h, w_in_ref[:, col_ga:col_ga + 2 * d_model])
    x1_ref[...] = _gated_merge_out(x, ya, gates[:, :d_model], gates[:, d_model:], o_scr[...],
                                   z_scr[:, 3 * hw:4 * hw], hgrn_norm_ref, w_hgrn_up_ref, w_out_ref,
                                   n_heads)


def _ffn_ple_kernel(xp_ref, xs_ref, pp_ref, ps_ref, g_ffn_ref, w_gate_ref, w_up_ref, w_down_ref,
                    g_ple_ref, w_ple_gate_ref, w_ple_proj_ref, g_final_ref, yp_ref, ys_ref, *, n_prompt):
    d_ff = w_gate_ref.shape[1]
    ff_cuts = list(range(0, d_ff, FFN_PIECE)) + [d_ff]

    def row_group(x_ref, p_ref, y_ref, rows):
        st = {"act": []}

        def norm():
            st["x"] = x_ref[rows, :]
            st["h2"] = _bf(_rms(st["x"], g_ffn_ref[...]))

        def ff(c0, c1):
            def run():
                g = _dot(st["h2"], w_gate_ref[:, c0:c1])
                st["act"].append(_bf(_silu(g) * _dot(st["h2"], w_up_ref[:, c0:c1])))
            return run

        def down():
            x = st["x"] + _dot(jnp.concatenate(st["act"], axis=1), w_down_ref[...])
            st["x"] = x
            st["h3"] = _bf(_rms(x, g_ple_ref[...]))

        def ple():
            gate = jax.nn.sigmoid(_dot(st["h3"], w_ple_gate_ref[...]))
            emb = _dot(_bf(p_ref[rows, :]), w_ple_proj_ref[...])
            y_ref[rows, :] = _rms(st["x"] + gate * emb, g_final_ref[...])

        return [[[norm]], [[ff(c0, c1) for c0, c1 in zip(ff_cuts, ff_cuts[1:])]], [[down]], [[ple]]]

    def tile(x_ref, p_ref, y_ref):
        cuts = [x_ref.shape[0] * c // FFN_ROW_CUTS[-1] for c in FFN_ROW_CUTS]
        _emit_staggered([row_group(x_ref, p_ref, y_ref, slice(r0, r1)) for r0, r1 in zip(cuts, cuts[1:])])

    i = pl.program_id(0)

    @pl.when(i < n_prompt)
    def _():
        tile(xp_ref, pp_ref, yp_ref)

    @pl.when(i >= n_prompt)
    def _():
        tile(xs_ref, ps_ref, ys_ref)


def _whole(_):
    return pl.BlockSpec(memory_space=pltpu.VMEM)


def _mixer_prompt(x, vectors, matrices, to_cast, *, tile, chunk):
    bsz, seq, d_model = x.shape
    pw = vectors[1].shape[1]
    hw = vectors[3].shape[1]
    n_heads = hw // LANES
    ns = PROMPT_STREAMS
    assert len(matrices) == N_MIXER_WEIGHTS
    assert seq % tile == 0 and tile % chunk == 0 and chunk % SUB == 0 and bsz % ns == 0
    nt = seq // tile
    n_steps = (bsz // ns) * nt

    def row_slab(w, exact):
        rep = 1 if w.shape[0] % (n_steps * 2 * SUBLANES) == 0 else 2
        rows = w.shape[0] * rep // n_steps
        assert rows * n_steps == w.shape[0] * rep and rows % (2 * SUBLANES) == 0 and (rep == 1 or not exact)
        return pl.BlockSpec((rows, w.shape[1]), lambda b, t, rep=rep: ((b * nt + t) // rep, 0))

    slabs = [row_slab(w, False) for w in to_cast]
    w_slabs = [row_slab(w, True) for w in matrices]
    wide_cols = max(w.shape[1] for w in matrices)
    tall_rows = STAGE_ROWS * wide_cols // d_model // STAGE_ROWS * STAGE_ROWS
    assert all(w.shape[0] % STAGE_ROWS == 0 and (w.shape[1] > d_model or w.shape[0] % tall_rows == 0)
               for w in matrices)
    outs = pl.pallas_call(
        functools.partial(_mixer_prompt_kernel, chunk=chunk, n_cast=len(to_cast)),
        grid=(bsz // ns, nt),
        in_specs=([pl.BlockSpec((ns, tile, d_model), lambda b, t: (b, t, 0))]
                  + [_whole(v) for v in vectors]
                  + [pl.BlockSpec(memory_space=pl.ANY) for _ in matrices] + slabs),
        out_specs=[
            pl.BlockSpec((ns, tile, d_model), lambda b, t: (b, t, 0)),
            pl.BlockSpec((ns, POOL_HALO, pw), lambda b, t: (b, 0, 0)),
            pl.BlockSpec((ns, n_heads, LANES, LANES), lambda b, t: (b, 0, 0, 0)),
        ] + w_slabs + slabs,
        out_shape=[
            jax.ShapeDtypeStruct((bsz, seq, d_model), F32),
            jax.ShapeDtypeStruct((bsz, POOL_HALO, pw), F32),
            jax.ShapeDtypeStruct((bsz, n_heads, LANES, LANES), F32),
        ] + [jax.ShapeDtypeStruct(w.shape, BF16) for w in matrices + to_cast],
        scratch_shapes=[pltpu.VMEM(w.shape, BF16) for w in matrices] + [
            pltpu.VMEM((2, tall_rows, d_model), F32),
            pltpu.VMEM((2, STAGE_ROWS, wide_cols), F32),
            pltpu.SemaphoreType.DMA((2,)),
            pltpu.VMEM((ns, 3, POOL_TOP + tile, pw), F32),
            pltpu.VMEM((ns, tile, 4 * hw), F32),
            pltpu.VMEM((ns, tile, hw), F32),
            pltpu.VMEM((ns, n_heads, LANES, LANES), F32),
        ],
        compiler_params=pltpu.CompilerParams(
            dimension_semantics=("arbitrary", "arbitrary"), vmem_limit_bytes=VMEM_LIMIT),
        name="mixer_prompt",
    )(x, *vectors, *matrices, *to_cast)
    n_w = len(matrices)
    return outs[0], outs[1], outs[2], outs[3:3 + n_w], outs[3 + n_w:]


def _mixer_sample(x, state_pool, state_hgrn, weights, *, tile_b):
    bsz, seq, d_model = x.shape
    buf, _, pw = state_pool.shape
    _, n_heads, dk, dv = state_hgrn.shape
    hw = n_heads * dv
    assert bsz % tile_b == 0 and seq == SUBLANES and buf >= max(POOL_WINDOWS) - 1
    assert dk == LANES and dv == LANES
    rows = tile_b * seq
    xf = x.reshape(bsz * seq, d_model)
    return pl.pallas_call(
        functools.partial(_mixer_sample_kernel, seq=seq),
        grid=(bsz // tile_b,),
        in_specs=[
            pl.BlockSpec((rows, d_model), lambda i: (i, 0)),
            pl.BlockSpec((buf, tile_b, pw), lambda i: (0, i, 0)),
            pl.BlockSpec((tile_b, n_heads, dk, dv), lambda i: (i, 0, 0, 0)),
        ] + [_whole(w) for w in weights],
        out_specs=[
            pl.BlockSpec((rows, d_model), lambda i: (i, 0)),
            pl.BlockSpec((buf, tile_b, pw), lambda i: (0, i, 0)),
            pl.BlockSpec((tile_b, n_heads, dk, dv), lambda i: (i, 0, 0, 0)),
        ],
        out_shape=[
            jax.ShapeDtypeStruct((bsz * seq, d_model), F32),
            jax.ShapeDtypeStruct((buf, bsz, pw), F32),
            jax.ShapeDtypeStruct((bsz, n_heads, dk, dv), F32),
        ],
        scratch_shapes=[
            pltpu.VMEM((pw // LANES, rows, LANES), F32),
            pltpu.VMEM((rows, 4 * hw), F32),
            pltpu.VMEM((rows, hw), F32),
            pltpu.VMEM((rows, hw), F32),
            pltpu.VMEM((rows, hw), F32),
            pltpu.VMEM((rows, hw), F32),
            pltpu.VMEM((rows, hw), F32),
        ],
        compiler_params=pltpu.CompilerParams(
            dimension_semantics=("parallel",), vmem_limit_bytes=VMEM_LIMIT),
        name="mixer_sample",
    )(xf, state_pool, state_hgrn, *weights)


def _ffn_ple(x_p, x_s, p_p, p_s, weights, *, tile):
    (n_p, d_model), n_s = x_p.shape, x_s.shape[0]
    p_dim = p_p.shape[1]
    assert n_p % tile == 0 and n_s % tile == 0
    tp, ts = n_p // tile, n_s // tile

    def prompt_idx(i):
        return (jnp.minimum(i, tp - 1), 0)

    def sample_idx(i):
        return (jnp.maximum(i - tp, 0), 0)

    return pl.pallas_call(
        functools.partial(_ffn_ple_kernel, n_prompt=tp),
        grid=(tp + ts,),
        in_specs=[
            pl.BlockSpec((tile, d_model), prompt_idx),
            pl.BlockSpec((tile, d_model), sample_idx),
            pl.BlockSpec((tile, p_dim), prompt_idx),
            pl.BlockSpec((tile, p_dim), sample_idx),
        ] + [_whole(w) for w in weights],
        out_specs=[
            pl.BlockSpec((tile, d_model), prompt_idx),
            pl.BlockSpec((tile, d_model), sample_idx),
        ],
        out_shape=[
            jax.ShapeDtypeStruct((n_p, d_model), F32),
            jax.ShapeDtypeStruct((n_s, d_model), F32),
        ],
        compiler_params=pltpu.CompilerParams(
            dimension_semantics=("arbitrary",), vmem_limit_bytes=VMEM_LIMIT),
        name="ffn_ple",
    )(x_p, x_s, p_p, p_s, *weights)


def kernel(x_prompt, x_sample, state_pool, state_hgrn, p_prompt, p_sample, g_mix, w_in, w_pool_mix, pool_scale, hgrn_lb, hgrn_norm, w_pool_up, w_hgrn_up, w_out, g_ffn, w_ffn_gate, w_ffn_up, w_ffn_down, g_ple, w_ple_gate, w_ple_proj, g_final):
    depth = w_in.shape[0]
    assert depth == 1 and hgrn_lb.shape[0] == 2
    bsz, seq, d_model = x_prompt.shape
    dbsz, dseq, _ = x_sample.shape
    buf = state_pool.shape[2]

    vectors = (g_mix, pool_scale, hgrn_lb, hgrn_norm)
    matrices = (w_in[0], w_pool_mix[0].reshape(-1, w_pool_mix.shape[-1]), w_pool_up[0], w_hgrn_up[0],
                w_out[0])
    x1_p, pool_p, hgrn_p, (wi, wpm, wpu, whu, wo), (wg, wu, wd, wpg, wpp) = _mixer_prompt(
        x_prompt, vectors, matrices,
        (w_ffn_gate[0], w_ffn_up[0], w_ffn_down[0], w_ple_gate[0], w_ple_proj[0]),
        tile=TOKEN_TILE, chunk=RECURRENCE_CHUNK)
    mixer_w = (g_mix, wi, wpm, pool_scale, hgrn_lb, hgrn_norm, wpu, whu, wo)
    ffn_w = (g_ffn, wg, wu, wd, g_ple, wpg, wpp, g_final.reshape(1, d_model))
    x1_s, pool_s, hgrn_s = _mixer_sample(x_sample, jnp.swapaxes(state_pool[0], 0, 1), state_hgrn[0],
                                         mixer_w, tile_b=SAMPLE_TILE_SEQS)

    y_p, y_s = _ffn_ple(x1_p.reshape(bsz * seq, d_model), x1_s,
                        p_prompt[0].reshape(bsz * seq, -1), p_sample[0].reshape(dbsz * dseq, -1),
                        ffn_w, tile=TOKEN_TILE)

    return (y_p.reshape(bsz, seq, d_model), y_s.reshape(dbsz, dseq, d_model),
            pool_p[None, :, POOL_HALO - buf:, :], hgrn_p[None],
            jnp.swapaxes(pool_s, 0, 1)[None], hgrn_s[None])
```
